```python
import math
import jax, jax.numpy as jnp
from jax import lax
import numpy as np

D_MODEL = 1024
BATCH = 1
SEQ = 16384
DEPTH = 1
DEC_BATCH = 128
DEC_SEQ = 1
PAST_LEN = 16384
PAGE_SIZE = 128

D_MIX = D_MODEL
N_HEADS_A = 8
QK_NOPE = 64
QK_ROPE = 32
QK_DIM = QK_NOPE + QK_ROPE
V_DIM = 64
D_ATTN = N_HEADS_A * V_DIM
Q_RANK = 384
KV_RANK = 256
ROPE_THETA = 10000.0
ATTN_SCALE = QK_DIM ** -0.5
Q_BLOCK = 128
D_SSM = D_MIX - D_ATTN
SSM_GROUP = 16
N_SSM_GROUPS = D_SSM // SSM_GROUP
SSM_STATE = 64
DT_MIN = 0.001
DT_MAX = 0.1
PEER_HEADS = 8
N_KEYS = 128
N_EXPERTS = N_KEYS * N_KEYS
PEER_TOPK = 16
PEER_QDIM = 256
PEER_HALF = PEER_QDIM // 2
PEER_BLOCK = 128
PLE_DIM = 256
EPS = 1e-6
D_IN = Q_RANK + KV_RANK + QK_ROPE + D_SSM

kernel_name = "hymba_mla_s5_peer_step"


def rmsnorm(x, g):
    xf = x.astype(jnp.float32)
    y = xf * lax.rsqrt(jnp.mean(xf * xf, axis=-1, keepdims=True) + EPS)
    return (y * g.astype(jnp.float32)).astype(x.dtype)


def rope_cos_sin(pos):
    inv = ROPE_THETA ** (-jnp.arange(0, QK_ROPE, 2, dtype=jnp.float32) / QK_ROPE)
    ang = pos.astype(jnp.float32)[:, None] * inv[None, :]
    return jnp.cos(ang), jnp.sin(ang)


def apply_rope(x, cos, sin):
    x1, x2 = jnp.split(x, 2, axis=-1)
    cos = cos.astype(x.dtype)
    sin = sin.astype(x.dtype)
    return jnp.concatenate([x1 * cos - x2 * sin, x1 * sin + x2 * cos], axis=-1)


def mixer_project(h, pos, W):
    proj = h @ W['w_in']
    c_q, c_kv, k_rope_raw, u_ssm = jnp.split(proj, [Q_RANK, Q_RANK + KV_RANK, Q_RANK + KV_RANK + QK_ROPE], axis=-1)
    cos, sin = rope_cos_sin(pos)
    q = jnp.einsum('btr,rhd->bthd', rmsnorm(c_q, W['g_q_lat']), W['w_uq'])
    q = rmsnorm(q, W['g_qn'])
    q_nope = q[..., :QK_NOPE]
    q_rope = apply_rope(q[..., QK_NOPE:], cos[:, None, :], sin[:, None, :])
    c_kv = rmsnorm(c_kv, W['g_kv_lat'])
    k_nope_raw = jnp.einsum('btc,chd->bthd', c_kv, W['w_uk'])
    kf = k_nope_raw.astype(jnp.float32)
    krf = k_rope_raw.astype(jnp.float32)
    ms = (jnp.sum(kf * kf, axis=-1) + jnp.sum(krf * krf, axis=-1)[..., None]) / QK_DIM
    k_scale = lax.rsqrt(ms + EPS).astype(h.dtype)
    k_rope = apply_rope(k_rope_raw * W['g_kn'][QK_NOPE:], cos, sin)
    return q_nope, q_rope, c_kv, k_nope_raw, k_rope, k_scale, u_ssm


def mla_prompt_attention(q_nope, q_rope, c_kv, k_nope_raw, k_rope, k_scale, W):
    B, T, H, _ = q_nope.shape
    ks = k_scale[..., None]
    k = jnp.concatenate([k_nope_raw * W['g_kn'][:QK_NOPE] * ks,
                         jnp.broadcast_to(k_rope[:, :, None, :], (B, T, H, QK_ROPE)) * ks], axis=-1)
    q = jnp.concatenate([q_nope, q_rope], axis=-1)
    v = jnp.einsum('btc,chd->bthd', c_kv, W['w_uv'])
    n_blocks = T // Q_BLOCK
    qb = q.reshape(B, n_blocks, Q_BLOCK, H, QK_DIM).transpose(1, 0, 2, 3, 4)
    k_pos = jnp.arange(T)

    def block(args):
        q_blk, start = args
        s = jnp.einsum('bqhd,bkhd->bhqk', q_blk, k).astype(jnp.float32) * ATTN_SCALE
        q_pos = start + jnp.arange(Q_BLOCK)
        s = jnp.where(k_pos[None, :] <= q_pos[:, None], s, -jnp.inf)
        p = jax.nn.softmax(s, axis=-1).astype(v.dtype)
        return jnp.einsum('bhqk,bkhd->bqhd', p, v)

    out = lax.map(block, (qb, jnp.arange(n_blocks) * Q_BLOCK))
    return out.transpose(1, 0, 2, 3, 4).reshape(B, T, H * V_DIM)


def mla_sample_attention(q_nope, q_rope, c_kv, k_rope, k_scale, past, W):
    lat_past, kr_past, ks_past = past
    B, S, H, _ = q_nope.shape
    q_lat = jnp.einsum('bshd,chd->bshc', q_nope * W['g_kn'][:QK_NOPE], W['w_uk'])

    def scores(lat, kr, ks):
        s = jnp.einsum('bshc,btc->bhst', q_lat, lat) + jnp.einsum('bshr,btr->bhst', q_rope, kr)
        return s.astype(jnp.float32) * ks.astype(jnp.float32).transpose(0, 2, 1)[:, :, None, :] * ATTN_SCALE

    s_past = scores(lat_past, kr_past, ks_past)
    causal = jnp.arange(S)[None, :] <= jnp.arange(S)[:, None]
    s_new = jnp.where(causal, scores(c_kv, k_rope, k_scale), -jnp.inf)
    p = jax.nn.softmax(jnp.concatenate([s_past, s_new], axis=-1), axis=-1).astype(lat_past.dtype)
    n_past = lat_past.shape[1]
    o_lat = (jnp.einsum('bhst,btc->bshc', p[..., :n_past], lat_past)
             + jnp.einsum('bhst,btc->bshc', p[..., n_past:], c_kv))
    out = jnp.einsum('bshc,chd->bshd', o_lat, W['w_uv'])
    return out.reshape(B, S, H * V_DIM)


def s5_mixer(u, h_prev_re, h_prev_im, W):
    B, T, _ = u.shape
    f32 = jnp.float32
    lam = lax.complex(W['ssm_a_re'].astype(f32), W['ssm_a_im'].astype(f32))
    dt = jnp.exp(W['ssm_log_dt'].astype(f32))[:, None]
    a_bar = jnp.exp(lam * dt)
    b_mat = lax.complex(W['ssm_b_re'].astype(f32), W['ssm_b_im'].astype(f32))
    b_bar = ((a_bar - 1.0) / lam)[:, :, None] * b_mat
    c_mat = lax.complex(W['ssm_c_re'].astype(f32), W['ssm_c_im'].astype(f32))
    ug = u.astype(f32).reshape(B, T, N_SSM_GROUPS, SSM_GROUP)
    bu = jnp.einsum('gpc,btgc->btgp', b_bar, ug.astype(jnp.complex64))
    h_prev = lax.complex(h_prev_re.astype(f32), h_prev_im.astype(f32))
    bu = bu.at[:, 0].add(a_bar * h_prev)
    a_seq = jnp.broadcast_to(a_bar, bu.shape)

    def combine(e1, e2):
        a1, b1 = e1
        a2, b2 = e2
        return a1 * a2, a2 * b1 + b2

    _, hs = lax.associative_scan(combine, (a_seq, bu), axis=1)
    y = jnp.einsum('gcp,btgp->btgc', c_mat, hs).real + W['ssm_d'].astype(f32) * ug
    y = y.reshape(B, T, D_SSM).astype(u.dtype)
    g = jax.nn.gelu(y)
    out = g * jax.nn.sigmoid(g @ W['w_glu'] + W['b_glu'])
    h_last = hs[:, -1]
    return out, h_last.real.astype(h_prev_re.dtype), h_last.imag.astype(h_prev_im.dtype)


def peer_ffn(h, W):
    shape = h.shape
    x2 = h.reshape(-1, D_MODEL)
    T = x2.shape[0]
    nb = -(-T // PEER_BLOCK)
    xb = jnp.pad(x2, ((0, nb * PEER_BLOCK - T), (0, 0))).reshape(nb, PEER_BLOCK, D_MODEL)
    w_q, sub_keys, u_tab, v_tab = W['w_peer_q'], W['peer_keys'], W['peer_u'], W['peer_v']

    def block(xt):
        q = (xt @ w_q).reshape(PEER_BLOCK, PEER_HEADS, 2, PEER_HALF).astype(jnp.float32)
        s = jnp.einsum('thid,hikd->thik', q, sub_keys.astype(jnp.float32))
        top_s, top_i = lax.top_k(s, PEER_TOPK)
        cand = top_s[:, :, 0, :, None] + top_s[:, :, 1, None, :]
        best_s, best_c = lax.top_k(cand.reshape(PEER_BLOCK, PEER_HEADS, PEER_TOPK * PEER_TOPK), PEER_TOPK)
        i1 = jnp.take_along_axis(top_i[:, :, 0], best_c // PEER_TOPK, axis=-1)
        i2 = jnp.take_along_axis(top_i[:, :, 1], best_c % PEER_TOPK, axis=-1)
        expert = i1 * N_KEYS + i2
        gate = jax.nn.softmax(best_s, axis=-1)
        act = jax.nn.gelu(jnp.einsum('thkd,td->thk', u_tab[expert], xt))
        return jnp.einsum('thk,thkd->td', (gate * act).astype(xt.dtype), v_tab[expert])

    yb = lax.map(block, xb)
    return yb.reshape(-1, D_MODEL)[:T].reshape(shape)


def layer(x, p, pos, past, h_prev_re, h_prev_im, W):
    h = rmsnorm(x, W['g_mix'])
    q_nope, q_rope, c_kv, k_nope_raw, k_rope, k_scale, u_ssm = mixer_project(h, pos, W)
    if past is None:
        attn = mla_prompt_attention(q_nope, q_rope, c_kv, k_nope_raw, k_rope, k_scale, W)
    else:
        attn = mla_sample_attention(q_nope, q_rope, c_kv, k_rope, k_scale, past, W)
    ssm, h_re, h_im = s5_mixer(u_ssm, h_prev_re, h_prev_im, W)
    mix = jnp.concatenate([rmsnorm(attn, W['g_attn_out']), rmsnorm(ssm, W['g_ssm_out'])], axis=-1)
    x = x + mix @ W['w_out']
    x = x + peer_ffn(rmsnorm(x, W['g_ffn']), W)
    x = x + (p @ W['w_ple']) * jax.nn.sigmoid(rmsnorm(x, W['g_ple']) @ W['w_ple_gate'])
    return x, c_kv, k_rope, k_scale, h_re, h_im


def setup_inputs(seed: int = 0) -> dict:
    key = jax.random.key(seed)
    ks = iter(jax.random.split(key, 48))
    f32 = jnp.float32

    def nrm(shape, scale):
        return jax.random.normal(next(ks), shape, f32) * scale

    def gain(n):
        return 1.0 + nrm((DEPTH, n), 0.02)

    n_pages = PAST_LEN // PAGE_SIZE
    n_used = DEC_BATCH * n_pages
    n_pool = n_used + max(1, n_used // 4)
    page_table = jax.random.permutation(next(ks), n_pool)[:n_used].reshape(DEC_BATCH, n_pages).astype(jnp.int32)
    a_im0 = math.pi * jnp.arange(SSM_STATE, dtype=f32)
    return {
        'x_prompt': nrm((BATCH, SEQ, D_MODEL), 1.0),
        'x_sample': nrm((DEC_BATCH, DEC_SEQ, D_MODEL), 1.0),
        'cache_kv_latent': nrm((DEPTH, n_pool, PAGE_SIZE, KV_RANK), 1.0),
        'cache_k_rope': nrm((DEPTH, n_pool, PAGE_SIZE, QK_ROPE), 1.0),
        'cache_k_scale': jnp.exp(nrm((DEPTH, n_pool, PAGE_SIZE, N_HEADS_A), 0.1)),
        'state_ssm_re': nrm((DEPTH, DEC_BATCH, N_SSM_GROUPS, SSM_STATE), 0.3),
        'state_ssm_im': nrm((DEPTH, DEC_BATCH, N_SSM_GROUPS, SSM_STATE), 0.3),
        'page_table': page_table,
        'p_prompt': nrm((DEPTH, BATCH, SEQ, PLE_DIM), 1.0),
        'p_sample': nrm((DEPTH, DEC_BATCH, DEC_SEQ, PLE_DIM), 1.0),
        'g_mix': gain(D_MODEL),
        'w_in': nrm((DEPTH, D_MODEL, D_IN), D_MODEL ** -0.5),
        'g_q_lat': gain(Q_RANK),
        'w_uq': nrm((DEPTH, Q_RANK, N_HEADS_A, QK_DIM), Q_RANK ** -0.5),
        'g_kv_lat': gain(KV_RANK),
        'w_uk': nrm((DEPTH, KV_RANK, N_HEADS_A, QK_NOPE), KV_RANK ** -0.5),
        'w_uv': nrm((DEPTH, KV_RANK, N_HEADS_A, V_DIM), KV_RANK ** -0.5),
        'g_qn': gain(QK_DIM),
        'g_kn': gain(QK_DIM),
        'ssm_a_re': -0.5 + nrm((DEPTH, N_SSM_GROUPS, SSM_STATE), 0.01),
        'ssm_a_im': a_im0 + nrm((DEPTH, N_SSM_GROUPS, SSM_STATE), 0.01),
        'ssm_log_dt': jax.random.uniform(next(ks), (DEPTH, N_SSM_GROUPS), f32, math.log(DT_MIN), math.log(DT_MAX)),
        'ssm_b_re': nrm((DEPTH, N_SSM_GROUPS, SSM_STATE, SSM_GROUP), (2 * SSM_GROUP) ** -0.5),
        'ssm_b_im': nrm((DEPTH, N_SSM_GROUPS, SSM_STATE, SSM_GROUP), (2 * SSM_GROUP) ** -0.5),
        'ssm_c_re': nrm((DEPTH, N_SSM_GROUPS, SSM_GROUP, SSM_STATE), SSM_STATE ** -0.5),
        'ssm_c_im': nrm((DEPTH, N_SSM_GROUPS, SSM_GROUP, SSM_STATE), SSM_STATE ** -0.5),
        'ssm_d': nrm((DEPTH, N_SSM_GROUPS, SSM_GROUP), 1.0),
        'w_glu': nrm((DEPTH, D_SSM, D_SSM), D_SSM ** -0.5),
        'b_glu': nrm((DEPTH, D_SSM), 0.02),
        'g_attn_out': gain(D_ATTN),
        'g_ssm_out': gain(D_SSM),
        'w_out': nrm((DEPTH, D_MIX, D_MODEL), D_MIX ** -0.5),
        'g_ffn': gain(D_MODEL),
        'w_peer_q': nrm((DEPTH, D_MODEL, PEER_HEADS * PEER_QDIM), D_MODEL ** -0.5),
        'peer_keys': nrm((DEPTH, PEER_HEADS, 2, N_KEYS, PEER_HALF), PEER_HALF ** -0.5),
        'peer_u': nrm((DEPTH, N_EXPERTS, D_MODEL), D_MODEL ** -0.5),
        'peer_v': nrm((DEPTH, N_EXPERTS, D_MODEL), PEER_HEADS ** -0.5),
        'g_ple': gain(D_MODEL),
        'w_ple': nrm((DEPTH, PLE_DIM, D_MODEL), PLE_DIM ** -0.5),
        'w_ple_gate': nrm((DEPTH, D_MODEL, D_MODEL), D_MODEL ** -0.5),
    }


def reference(x_prompt, x_sample, cache_kv_latent, cache_k_rope, cache_k_scale, state_ssm_re, state_ssm_im,
              page_table, p_prompt, p_sample, g_mix, w_in, g_q_lat, w_uq, g_kv_lat, w_uk, w_uv, g_qn, g_kn,
              ssm_a_re, ssm_a_im, ssm_log_dt, ssm_b_re, ssm_b_im, ssm_c_re, ssm_c_im, ssm_d, w_glu, b_glu,
              g_attn_out, g_ssm_out, w_out, g_ffn, w_peer_q, peer_keys, peer_u, peer_v, g_ple, w_ple, w_ple_gate):
    n_dec = x_sample.shape[0]
    pos_prompt = jnp.arange(x_prompt.shape[1])
    zero_state = jnp.zeros((x_prompt.shape[0], N_SSM_GROUPS, SSM_STATE), state_ssm_re.dtype)
    x_p, x_s = x_prompt, x_sample
    lat_p, kr_p, ks_p, re_p, im_p = [], [], [], [], []
    lat_s, kr_s, ks_s, re_s, im_s = [], [], [], [], []
    for i in range(DEPTH):
        W = {'g_mix': g_mix[i], 'w_in': w_in[i], 'g_q_lat': g_q_lat[i], 'w_uq': w_uq[i], 'g_kv_lat': g_kv_lat[i],
             'w_uk': w_uk[i], 'w_uv': w_uv[i], 'g_qn': g_qn[i], 'g_kn': g_kn[i], 'ssm_a_re': ssm_a_re[i],
             'ssm_a_im': ssm_a_im[i], 'ssm_log_dt': ssm_log_dt[i], 'ssm_b_re': ssm_b_re[i], 'ssm_b_im': ssm_b_im[i],
             'ssm_c_re': ssm_c_re[i], 'ssm_c_im': ssm_c_im[i], 'ssm_d': ssm_d[i], 'w_glu': w_glu[i], 'b_glu': b_glu[i],
             'g_attn_out': g_attn_out[i], 'g_ssm_out': g_ssm_out[i], 'w_out': w_out[i], 'g_ffn': g_ffn[i],
             'w_peer_q': w_peer_q[i], 'peer_keys': peer_keys[i], 'peer_u': peer_u[i], 'peer_v': peer_v[i],
             'g_ple': g_ple[i], 'w_ple': w_ple[i], 'w_ple_gate': w_ple_gate[i]}
        x_p, a, b, c, d, e = layer(x_p, p_prompt[i], pos_prompt, None, zero_state, zero_state, W)
        lat_p.append(a); kr_p.append(b); ks_p.append(c); re_p.append(d); im_p.append(e)
        lat_past = cache_kv_latent[i][page_table].reshape(n_dec, -1, KV_RANK)
        kr_past = cache_k_rope[i][page_table].reshape(n_dec, -1, QK_ROPE)
        ks_past = cache_k_scale[i][page_table].reshape(n_dec, -1, N_HEADS_A)
        pos_sample = lat_past.shape[1] + jnp.arange(x_sample.shape[1])
        x_s, a, b, c, d, e = layer(x_s, p_sample[i], pos_sample, (lat_past, kr_past, ks_past),
                                   state_ssm_re[i], state_ssm_im[i], W)
        lat_s.append(a); kr_s.append(b); ks_s.append(c); re_s.append(d); im_s.append(e)
    return (x_p, x_s,
            jnp.stack(lat_p), jnp.stack(kr_p), jnp.stack(ks_p), jnp.stack(re_p), jnp.stack(im_p),
            jnp.stack(lat_s), jnp.stack(kr_s), jnp.stack(ks_s), jnp.stack(re_s), jnp.stack(im_s))
```

```python
import functools
import math

import jax
import jax.numpy as jnp
import numpy as np
from jax import lax
from jax.experimental import pallas as pl
from jax.experimental.pallas import tpu as pltpu

F32 = jnp.float32
BF16 = jnp.bfloat16

D_MODEL = 1024
N_HEADS = 8
QK_NOPE = 64
QK_ROPE = 32
QK_DIM = QK_NOPE + QK_ROPE
V_DIM = 64
D_ATTN = N_HEADS * V_DIM
Q_RANK = 384
KV_RANK = 256
ROPE_THETA = 10000.0
ATTN_SCALE = QK_DIM ** -0.5
D_SSM = 512
SSM_GROUP = 16
N_GROUPS = 32
SSM_STATE = 64
N_STATE = N_GROUPS * SSM_STATE
PEER_HEADS = 8
N_KEYS = 128
N_EXPERTS = N_KEYS * N_KEYS
PEER_TOPK = 16
PEER_HALF = 128
PLE_DIM = 256
PAGE = 128
EPS = 1e-6

LANES = 128
HP = N_HEADS * LANES
NEG = -1e30
VMEM_LIMIT = 56 * 1024 * 1024


def _cparams(sem, vmem=VMEM_LIMIT):
    return pltpu.CompilerParams(dimension_semantics=sem, vmem_limit_bytes=vmem)


def _dot(a, b):
    return jnp.dot(a, b, preferred_element_type=F32)


def _dot_nt(a, b):
    return lax.dot_general(a, b, (((1,), (1,)), ((), ())), preferred_element_type=F32)


def _dot_hilo(x, w):
    hi = x.astype(BF16)
    lo = (x - hi.astype(F32)).astype(BF16)
    return _dot(hi, w) + _dot(lo, w)


def _dot_hml(x, w):
    hi = x.astype(BF16)
    r1 = x - hi.astype(F32)
    mid = r1.astype(BF16)
    lo = (r1 - mid.astype(F32)).astype(BF16)
    return _dot(hi, w) + _dot(mid, w) + _dot(lo, w)


def _rms(x, g, n):
    ms = jnp.sum(x * x, axis=-1, keepdims=True) * (1.0 / n)
    return x * lax.rsqrt(ms + EPS) * g


def _gelu(x):
    return 0.5 * x * (1.0 + jnp.tanh(0.7978845608028654 * (x + 0.044715 * (x * x * x))))


def _sigmoid(x):
    return 1.0 / (1.0 + jnp.exp(-x))


def _full(shape):
    return pl.BlockSpec(shape, lambda *_: (0,) * len(shape))


def _proj_kernel(x_ref, cos_ref, sin_ref, gmix_ref, win_ref, gq_ref, wuq_ref, wuqs_ref, gqn_ref, gqns_ref,
                 ind_ref, exp_ref, gkv_ref, wuk_ref, wuv_ref, gknn_ref, gkc_ref, gks_ref,
                 q_ref, k_ref, v_ref, ckv_ref, krope_ref, kscale_ref, u_ref):
    x = x_ref[...]
    cos = cos_ref[...]
    sin = sin_ref[...]
    h = _rms(x, gmix_ref[...], D_MODEL).astype(BF16)
    proj = _dot(h, win_ref[...])
    cq = proj[:, :Q_RANK]
    ckv = proj[:, Q_RANK:Q_RANK + KV_RANK]
    u_ref[...] = proj[:, 640:1152]
    krt = proj[:, 1152:1280]
    krts = proj[:, 1280:1408]

    cqn = _rms(cq, gq_ref[...], Q_RANK).astype(BF16)
    q_raw = _dot(cqn, wuq_ref[...])
    q_sw = _dot(cqn, wuqs_ref[...])
    ind = ind_ref[...]
    expd = exp_ref[...]
    ssq = _dot_hilo(q_raw * q_raw, ind)
    r_exp = _dot_hilo(lax.rsqrt(ssq * (1.0 / QK_DIM) + EPS), expd)
    qn = q_raw * r_exp
    qns = q_sw * r_exp
    gqn = gqn_ref[...]
    gqns = gqns_ref[...]

    ckvn = _rms(ckv, gkv_ref[...], KV_RANK)
    ckv_ref[...] = ckvn
    ckvb = ckvn.astype(BF16)
    kn_raw = _dot(ckvb, wuk_ref[...])
    v_ref[...] = _dot(ckvb, wuv_ref[...]).astype(v_ref.dtype)
    ssk = _dot_hilo(kn_raw * kn_raw, ind)
    ms_rope = jnp.sum(krt * krt, axis=-1, keepdims=True)
    ks = lax.rsqrt((ssk + ms_rope) * (1.0 / QK_DIM) + EPS)
    kscale_ref[...] = ks[:, :N_HEADS]
    ks_exp = _dot_hilo(ks, expd)
    kr_tile = (krt * gkc_ref[...]) * cos + (krts * gks_ref[...]) * sin
    krope_ref[...] = kr_tile[:, QK_NOPE:QK_DIM]
    gknn = gknn_ref[...]
    for hd in range(N_HEADS):
        sl = slice(hd * LANES, (hd + 1) * LANES)
        q_ref[:, sl] = ((qn[:, sl] * gqn) * cos + (qns[:, sl] * gqns) * sin).astype(q_ref.dtype)
        k_ref[:, sl] = ((kn_raw[:, sl] * gknn + kr_tile) * ks_exp[:, sl]).astype(k_ref.dtype)


def _project(x, cos_t, sin_t, W, q_dtype, tm):
    T = x.shape[0]
    tok = lambda w: pl.BlockSpec((tm, w), lambda i: (i, 0))
    weights = [W['g_mix'], W['w_in_p'], W['g_q_lat'], W['w_uq_p'], W['w_uq_sw'], W['g_qn_p'], W['g_qn_sw'],
               W['ind'], W['expand'], W['g_kv_lat'], W['w_uk_p'], W['w_uv_p'], W['g_kn_nope'], W['g_kc'], W['g_ks']]
    out_shape = [jax.ShapeDtypeStruct((T, HP), q_dtype), jax.ShapeDtypeStruct((T, HP), BF16),
                 jax.ShapeDtypeStruct((T, HP), BF16), jax.ShapeDtypeStruct((T, KV_RANK), F32),
                 jax.ShapeDtypeStruct((T, QK_ROPE), F32), jax.ShapeDtypeStruct((T, N_HEADS), F32),
                 jax.ShapeDtypeStruct((T, D_SSM), F32)]
    return pl.pallas_call(
        _proj_kernel, grid=(T // tm,),
        in_specs=[tok(D_MODEL), tok(LANES), tok(LANES)] + [_full(w.shape) for w in weights],
        out_specs=[tok(HP), tok(HP), tok(HP), tok(KV_RANK), tok(QK_ROPE), tok(N_HEADS), tok(D_SSM)],
        out_shape=out_shape, compiler_params=_cparams(("parallel",)), name="project",
    )(x, cos_t, sin_t, *weights)


def _flash_kernel(q_ref, k_ref, v_ref, o_ref, m_sc, l_sc, acc_sc, *, tq, tk):
    i = pl.program_id(1)
    q = q_ref[...]
    m_sc[...] = jnp.full(m_sc.shape, NEG, F32)
    l_sc[...] = jnp.zeros(l_sc.shape, F32)
    acc_sc[...] = jnp.zeros(acc_sc.shape, F32)

    def step(j, masked):
        start = pl.multiple_of(j * tk, tk)
        k = k_ref[pl.ds(start, tk), :]
        v = v_ref[pl.ds(start, tk), :]
        s = _dot_nt(q, k) * ATTN_SCALE
        if masked:
            rows = i * tq + lax.broadcasted_iota(jnp.int32, (tq, tk), 0)
            cols = j * tk + lax.broadcasted_iota(jnp.int32, (tq, tk), 1)
            s = jnp.where(cols <= rows, s, NEG)
        m_prev = m_sc[...]
        m_new = jnp.maximum(m_prev, jnp.max(s, axis=-1, keepdims=True))
        alpha = jnp.exp(m_prev - m_new)
        p = jnp.exp(s - m_new)
        l_sc[...] = alpha * l_sc[...] + jnp.sum(p, axis=-1, keepdims=True)
        acc_sc[...] = alpha * acc_sc[...] + _dot(p.astype(BF16), v)
        m_sc[...] = m_new

    n_full = i * (tq // tk)
    lax.fori_loop(0, n_full, lambda j, c: (step(j, False), c)[1], 0)
    for jj in range(tq // tk):
        step(n_full + jj, True)
    o_ref[...] = acc_sc[...] / l_sc[...]


def _flash_attention(q, k, v, tq, tk):
    T = q.shape[0]
    kern = functools.partial(_flash_kernel, tq=tq, tk=tk)
    return pl.pallas_call(
        kern, grid=(N_HEADS, T // tq),
        in_specs=[pl.BlockSpec((tq, LANES), lambda h, i: (i, h)),
                  pl.BlockSpec((T, LANES), lambda h, i: (0, h)),
                  pl.BlockSpec((T, LANES), lambda h, i: (0, h))],
        out_specs=pl.BlockSpec((tq, LANES), lambda h, i: (i, h)),
        out_shape=jax.ShapeDtypeStruct((T, HP), F32),
        scratch_shapes=[pltpu.VMEM((tq, 1), F32), pltpu.VMEM((tq, 1), F32), pltpu.VMEM((tq, LANES), F32)],
        compiler_params=_cparams(("parallel", "arbitrary")), name="flash_attention",
    )(q, k, v)


def _cmul(ar, ai, br, bi):
    return ar * br - ai * bi, ar * bi + ai * br


def _ssm_disc_kernel(are_ref, aim_ref, dt_ref, bre_ref, bim_ref, abr_ref, abi_ref, bbr_ref, bbi_ref):
    lr = are_ref[...]
    li = aim_ref[...]
    dt = dt_ref[...]
    mag = jnp.exp(lr * dt)
    abr = mag * jnp.cos(li * dt)
    abi = mag * jnp.sin(li * dt)
    abr_ref[...] = abr
    abi_ref[...] = abi
    inv = 1.0 / (lr * lr + li * li)
    cr, ci = _cmul(abr - 1.0, abi, lr * inv, -li * inv)
    bbr, bbi = _cmul(cr, ci, bre_ref[...], bim_ref[...])
    bbr_ref[...] = bbr
    bbi_ref[...] = bbi


def _ssm_discretise(P):
    col = lambda a: a.reshape(N_STATE, 1)
    dt = jnp.repeat(jnp.exp(P['ssm_log_dt']), SSM_STATE).reshape(N_STATE, 1)
    bre = P['ssm_b_re'].reshape(N_STATE, SSM_GROUP)
    bim = P['ssm_b_im'].reshape(N_STATE, SSM_GROUP)
    s1 = jax.ShapeDtypeStruct((N_STATE, 1), F32)
    s16 = jax.ShapeDtypeStruct((N_STATE, SSM_GROUP), F32)
    return pl.pallas_call(_ssm_disc_kernel, out_shape=[s1, s1, s16, s16], name="ssm_discretise")(
        col(P['ssm_a_re']), col(P['ssm_a_im']), dt, bre, bim)


def _prep_ssm(P):
    abr, abi, bbr, bbi = _ssm_discretise(P)
    eye = jnp.eye(N_GROUPS, dtype=F32)

    def bdiag_in(b):
        b = b.reshape(N_GROUPS, SSM_STATE, SSM_GROUP)
        return jnp.einsum('gpc,gh->gchp', b, eye).reshape(D_SSM, N_STATE)

    def bdiag_out(c):
        return jnp.einsum('gcp,gh->gphc', c, eye).reshape(N_STATE, D_SSM)

    W = {}
    W['ssm_bm'] = jnp.concatenate([bdiag_in(bbr), bdiag_in(bbi)], axis=1)
    W['ssm_cm'] = jnp.concatenate([bdiag_out(P['ssm_c_re']), -bdiag_out(P['ssm_c_im'])], axis=0)
    W['ssm_ar'] = abr.reshape(1, N_STATE)
    W['ssm_ai'] = abi.reshape(1, N_STATE)
    W['ssm_d'] = P['ssm_d'].reshape(1, D_SSM)
    W['w_glu'] = P['w_glu'].astype(BF16)
    W['b_glu'] = P['b_glu'].reshape(1, D_SSM)
    return W


SCAN_LANES = 512


def _s5_prompt_kernel(u_ref, perm_ref, permt_ref, bm_ref, cm_ref, ar_ref, ai_ref, d_ref, wglu_ref, bglu_ref,
                      out_ref, hre_ref, him_ref, x_sc, carry_sc, *, tb):
    n = tb // 8
    nk = N_STATE // SCAN_LANES
    step = pl.program_id(0)

    @pl.when(step == 0)
    def _():
        carry_sc[...] = jnp.zeros(carry_sc.shape, F32)

    u = u_ref[...]
    up = _dot(perm_ref[...], u.astype(BF16)).astype(BF16)
    for k in range(nk):
        uk = up[:, k * LANES:(k + 1) * LANES]
        for c in range(2):
            lo = c * N_STATE + k * SCAN_LANES
            x_sc[:, lo:lo + SCAN_LANES] = _dot(uk, bm_ref[k * LANES:(k + 1) * LANES, lo:lo + SCAN_LANES])

    for k in range(nk):
        re_sl = slice(k * SCAN_LANES, (k + 1) * SCAN_LANES)
        im_sl = slice(N_STATE + k * SCAN_LANES, N_STATE + (k + 1) * SCAN_LANES)
        ar = jnp.broadcast_to(ar_ref[:, re_sl], (8, SCAN_LANES))
        ai = jnp.broadcast_to(ai_ref[:, re_sl], (8, SCAN_LANES))

        def local_scan(i, h):
            rows = pl.ds(pl.multiple_of(i * 8, 8), 8)
            pr, pi = _cmul(ar, ai, h[0], h[1])
            hr = pr + x_sc[rows, re_sl]
            hi = pi + x_sc[rows, im_sl]
            x_sc[rows, re_sl] = hr
            x_sc[rows, im_sl] = hi
            return hr, hi

        zero = jnp.zeros((8, SCAN_LANES), F32)
        er, ei = lax.fori_loop(0, n, local_scan, (zero, zero))

        pr, pi = ar, ai
        for _ in range(int(math.log2(n))):
            pr, pi = _cmul(pr, pi, pr, pi)
        cr = [carry_sc[0:1, re_sl]]
        ci = [carry_sc[1:2, re_sl]]
        for s in range(1, 9):
            qr, qi = _cmul(pr[0:1], pi[0:1], cr[-1], ci[-1])
            cr.append(qr + er[s - 1:s])
            ci.append(qi + ei[s - 1:s])
        carry_sc[0:1, re_sl] = cr[8]
        carry_sc[1:2, re_sl] = ci[8]
        cin = (jnp.concatenate(cr[:8], axis=0), jnp.concatenate(ci[:8], axis=0))

        def patch(i, c):
            rows = pl.ds(pl.multiple_of(i * 8, 8), 8)
            c = _cmul(ar, ai, c[0], c[1])
            x_sc[rows, re_sl] = x_sc[rows, re_sl] + c[0]
            x_sc[rows, im_sl] = x_sc[rows, im_sl] + c[1]
            return c

        lax.fori_loop(0, n, patch, cin)

    ys = []
    for k in range(nk):
        zr = x_sc[:, k * SCAN_LANES:(k + 1) * SCAN_LANES].astype(BF16)
        zi = x_sc[:, N_STATE + k * SCAN_LANES:N_STATE + (k + 1) * SCAN_LANES].astype(BF16)
        ys.append(_dot(zr, cm_ref[k * SCAN_LANES:(k + 1) * SCAN_LANES, k * LANES:(k + 1) * LANES])
                  + _dot(zi, cm_ref[N_STATE + k * SCAN_LANES:N_STATE + (k + 1) * SCAN_LANES,
                                    k * LANES:(k + 1) * LANES]))
    yp = jnp.concatenate(ys, axis=1)
    pt = permt_ref[...]
    hi = yp.astype(BF16)
    r1 = yp - hi.astype(F32)
    mid = r1.astype(BF16)
    lo = (r1 - mid.astype(F32)).astype(BF16)
    y = _dot(pt, hi) + _dot(pt, mid) + _dot(pt, lo) + d_ref[...] * u
    g = _gelu(y)
    out_ref[...] = g * _sigmoid(_dot(g.astype(BF16), wglu_ref[...]) + bglu_ref[...])
    hre_ref[...] = carry_sc[0:1, :]
    him_ref[...] = carry_sc[1:2, :]


def _s5_prompt(u, W, tb):
    T = u.shape[0]
    n = tb // 8
    r = np.arange(tb)
    src = (r % 8) * n + r // 8
    perm = np.zeros((tb, tb), np.float32)
    perm[r, src] = 1.0
    kern = functools.partial(_s5_prompt_kernel, tb=tb)
    weights = [jnp.asarray(perm, BF16), jnp.asarray(perm.T, BF16), W['ssm_bm'].astype(BF16),
               W['ssm_cm'].astype(BF16), W['ssm_ar'], W['ssm_ai'], W['ssm_d'], W['w_glu'], W['b_glu']]
    return pl.pallas_call(
        kern, grid=(T // tb,),
        in_specs=[pl.BlockSpec((tb, D_SSM), lambda i: (i, 0))] + [_full(w.shape) for w in weights],
        out_specs=[pl.BlockSpec((tb, D_SSM), lambda i: (i, 0)), _full((1, N_STATE)), _full((1, N_STATE))],
        out_shape=[jax.ShapeDtypeStruct((T, D_SSM), F32), jax.ShapeDtypeStruct((1, N_STATE), F32),
                   jax.ShapeDtypeStruct((1, N_STATE), F32)],
        scratch_shapes=[pltpu.VMEM((tb, 2 * N_STATE), F32), pltpu.VMEM((2, N_STATE), F32)],
        compiler_params=_cparams(("arbitrary",)), name="s5_prompt",
    )(u, *weights)


def _s5_step_kernel(u_ref, hre_ref, him_ref, bm_ref, cm_ref, ar_ref, ai_ref, d_ref, wglu_ref, bglu_ref,
                    out_ref, nre_ref, nim_ref):
    u = u_ref[...]
    bm = bm_ref[...]
    uh = u.astype(BF16)
    ul = (u - uh.astype(F32)).astype(BF16)
    bh = bm.astype(BF16)
    bl = (bm - bh.astype(F32)).astype(BF16)
    bu = _dot(uh, bh) + _dot(uh, bl) + _dot(ul, bh)
    pr, pi = _cmul(ar_ref[...], ai_ref[...], hre_ref[...], him_ref[...])
    hr = pr + bu[:, :N_STATE]
    hi = pi + bu[:, N_STATE:]
    nre_ref[...] = hr
    nim_ref[...] = hi
    cm = cm_ref[...]
    y = _dot(hr.astype(BF16), cm[:N_STATE]) + _dot(hi.astype(BF16), cm[N_STATE:]) + d_ref[...] * u
    g = _gelu(y)
    out_ref[...] = g * _sigmoid(_dot(g.astype(BF16), wglu_ref[...]) + bglu_ref[...])


def _s5_step(u, hre, him, W):
    B = u.shape[0]
    st = jax.ShapeDtypeStruct((B, N_STATE), F32)
    return pl.pallas_call(
        _s5_step_kernel, out_shape=[jax.ShapeDtypeStruct((B, D_SSM), F32), st, st],
        compiler_params=pltpu.CompilerParams(vmem_limit_bytes=VMEM_LIMIT), name="s5_step",
    )(u, hre, him, W['ssm_bm'], W['ssm_cm'].astype(BF16), W['ssm_ar'], W['ssm_ai'], W['ssm_d'],
      W['w_glu'], W['b_glu'])


def _qlat_kernel(q_ref, gknn_ref, wukt_ref, o_ref):
    g = gknn_ref[...]
    for h in range(N_HEADS):
        qh = (q_ref[:, h * LANES:(h + 1) * LANES] * g).astype(BF16)
        o_ref[:, h * KV_RANK:(h + 1) * KV_RANK] = _dot(qh, wukt_ref[h]).astype(o_ref.dtype)


def _uv_kernel(ol_ref, wuv_ref, o_ref):
    for h in range(N_HEADS):
        o_ref[:, h * LANES:(h + 1) * LANES] = _dot(ol_ref[:, h * KV_RANK:(h + 1) * KV_RANK].astype(BF16),
                                                   wuv_ref[:, h * LANES:(h + 1) * LANES])


def _decode_kernel(pt_ref, qlat_ref, qr_ref, ckv_ref, krn_ref, ksn_ref, eye_ref, lat_hbm, kr_hbm, ks_hbm, o_ref,
                   lat_buf, kr_buf, ks_buf, sem, m_sc, l_sc, acc_sc, *, pg, nch):
    b = pl.program_id(0)
    c = pl.program_id(1)
    g = b * nch + c
    total = pl.num_programs(0) * nch

    def copies(step, slot):
        out = []
        for p in range(pg):
            page = pt_ref[step * pg + p]
            rows = pl.ds(p * PAGE, PAGE)
            out.append(pltpu.make_async_copy(lat_hbm.at[page], lat_buf.at[slot, rows], sem.at[slot, 0]))
            out.append(pltpu.make_async_copy(kr_hbm.at[page], kr_buf.at[slot, rows], sem.at[slot, 1]))
            out.append(pltpu.make_async_copy(ks_hbm.at[page], ks_buf.at[slot, rows], sem.at[slot, 2]))
        return out

    @pl.when(g == 0)
    def _():
        for d in copies(0, 0):
            d.start()

    @pl.when(g + 1 < total)
    def _():
        for d in copies(g + 1, (g + 1) % 2):
            d.start()

    @pl.when(c == 0)
    def _():
        m_sc[...] = jnp.full(m_sc.shape, NEG, F32)
        l_sc[...] = jnp.zeros(l_sc.shape, F32)
        acc_sc[...] = jnp.zeros(acc_sc.shape, F32)

    slot = g % 2
    for d in copies(g, slot):
        d.wait()

    qlat = qlat_ref[0]
    lat = lat_buf[slot].astype(BF16)
    s = _dot_nt(qlat, lat) + _dot_nt(qr_ref[0], kr_buf[slot].astype(BF16))
    ks = ks_buf[slot]
    eye = eye_ref[...]
    kh = ks.astype(BF16)
    k1 = ks - kh.astype(F32)
    km = k1.astype(BF16)
    kl = (k1 - km.astype(F32)).astype(BF16)
    ks_t = _dot_nt(eye, kh) + _dot_nt(eye, km) + _dot_nt(eye, kl)
    s = s * ks_t * ATTN_SCALE
    m_prev = m_sc[...]
    m_new = jnp.maximum(m_prev, jnp.max(s, axis=-1, keepdims=True))
    alpha = jnp.exp(m_prev - m_new)
    p = jnp.exp(s - m_new)
    l_sc[...] = alpha * l_sc[...] + jnp.sum(p, axis=-1, keepdims=True)
    acc_sc[...] = alpha * acc_sc[...] + _dot(p.astype(BF16), lat)
    m_sc[...] = m_new

    @pl.when(c == nch - 1)
    def _():
        ckv = ckv_ref[0]
        s_new = (jnp.sum(qlat.astype(F32) * ckv, axis=-1, keepdims=True)
                 + jnp.sum(qr_ref[0].astype(F32) * krn_ref[0], axis=-1, keepdims=True))
        s_new = s_new * ksn_ref[0] * ATTN_SCALE
        m_prev = m_sc[...]
        m_new = jnp.maximum(m_prev, s_new)
        alpha = jnp.exp(m_prev - m_new)
        p_new = jnp.exp(s_new - m_new)
        l = alpha * l_sc[...] + p_new
        o_ref[0] = (alpha * acc_sc[...] + p_new * ckv) / l


def _decode_attention(page_table, qlat, qr, ckv, krn, ksn, lat_pool, kr_pool, ks_pool, pg):
    B, n_pages = page_table.shape
    nch = n_pages // pg
    tk = pg * PAGE
    kern = functools.partial(_decode_kernel, pg=pg, nch=nch)
    eye = jnp.eye(N_HEADS, dtype=BF16)
    per_b = lambda shape: pl.BlockSpec((1,) + shape, lambda b, c, pt: (b, 0, 0))
    grid_spec = pltpu.PrefetchScalarGridSpec(
        num_scalar_prefetch=1, grid=(B, nch),
        in_specs=[per_b((N_HEADS, KV_RANK)), per_b((N_HEADS, QK_ROPE)), per_b((1, KV_RANK)), per_b((1, QK_ROPE)),
                  per_b((N_HEADS, 1)), pl.BlockSpec((N_HEADS, N_HEADS), lambda b, c, pt: (0, 0)),
                  pl.BlockSpec(memory_space=pl.ANY), pl.BlockSpec(memory_space=pl.ANY),
                  pl.BlockSpec(memory_space=pl.ANY)],
        out_specs=per_b((N_HEADS, KV_RANK)),
        scratch_shapes=[pltpu.VMEM((2, tk, KV_RANK), F32), pltpu.VMEM((2, tk, QK_ROPE), F32),
                        pltpu.VMEM((2, tk, N_HEADS), F32), pltpu.SemaphoreType.DMA((2, 3)),
                        pltpu.VMEM((N_HEADS, 1), F32), pltpu.VMEM((N_HEADS, 1), F32),
                        pltpu.VMEM((N_HEADS, KV_RANK), F32)])
    return pl.pallas_call(
        kern, grid_spec=grid_spec, out_shape=jax.ShapeDtypeStruct((B, N_HEADS, KV_RANK), F32),
        compiler_params=_cparams(("arbitrary", "arbitrary")), name="decode_attention",
    )(page_table.reshape(-1), qlat, qr, ckv, krn, ksn, eye, lat_pool, kr_pool, ks_pool)


def _sample_attention(q, ckv, krope, kscale, page_table, lat_pool, kr_pool, ks_pool, W, pg):
    B = q.shape[0]
    qlat = pl.pallas_call(_qlat_kernel, out_shape=jax.ShapeDtypeStruct((B, N_HEADS * KV_RANK), BF16),
                          name="q_absorb")(q, W['g_kn_nope'], W['w_uk_t'])
    qr = q.reshape(B, N_HEADS, LANES)[:, :, QK_NOPE:QK_DIM].astype(BF16)
    ol = _decode_attention(page_table, qlat.reshape(B, N_HEADS, KV_RANK), qr, ckv.reshape(B, 1, KV_RANK),
                           krope.reshape(B, 1, QK_ROPE), kscale.reshape(B, N_HEADS, 1),
                           lat_pool, kr_pool, ks_pool, pg)
    return pl.pallas_call(_uv_kernel, out_shape=jax.ShapeDtypeStruct((B, HP), F32),
                          name="value_up")(ol.reshape(B, N_HEADS * KV_RANK), W['w_uv_p'])


def _mix_kernel(attn_ref, ssm_ref, x_ref, ga_ref, gs_ref, woa_ref, wos_ref, gffn_ref, wq_ref,
                x1_ref, xn_ref, qp_ref):
    a = _rms(attn_ref[...], ga_ref[...], D_ATTN).astype(BF16)
    s = _rms(ssm_ref[...], gs_ref[...], D_SSM).astype(BF16)
    x1 = x_ref[...] + _dot(a, woa_ref[...]) + _dot(s, wos_ref[...])
    x1_ref[...] = x1
    xn = _rms(x1, gffn_ref[...], D_MODEL).astype(BF16)
    xn_ref[...] = xn
    qp_ref[...] = _dot(xn, wq_ref[...]).astype(BF16)


def _mix(attn, ssm, x, W, tm):
    T = x.shape[0]
    tok = lambda w: pl.BlockSpec((tm, w), lambda i: (i, 0))
    weights = [W['g_attn_p'], W['g_ssm_out'], W['w_out_a'], W['w_out_s'], W['g_ffn'], W['w_peer_q']]
    nq = 2 * PEER_HEADS * PEER_HALF
    return pl.pallas_call(
        _mix_kernel, grid=(T // tm,),
        in_specs=[tok(HP), tok(D_SSM), tok(D_MODEL)] + [_full(w.shape) for w in weights],
        out_specs=[tok(D_MODEL), tok(D_MODEL), tok(nq)],
        out_shape=[jax.ShapeDtypeStruct((T, D_MODEL), F32), jax.ShapeDtypeStruct((T, D_MODEL), BF16),
                   jax.ShapeDtypeStruct((T, nq), BF16)],
        compiler_params=_cparams(("parallel",)), name="mix",
    )(attn, ssm, x, *weights)


_CAND = [(a, b) for a in range(PEER_TOPK) for b in range(PEER_TOPK) if (a + 1) * (b + 1) <= PEER_TOPK]


def _top16_rows(s):
    vals = []
    for _ in range(PEER_TOPK):
        m = jnp.max(s, axis=0, keepdims=True)
        vals.append(m)
        s = jnp.where(s == m, NEG, s)
    return vals


def _route_kernel(qp_ref, keys_ref, s1_ref, e1_ref, s2_ref, misc_ref):
    for h in range(PEER_HEADS):
        q1 = qp_ref[:, (2 * h) * PEER_HALF:(2 * h + 1) * PEER_HALF]
        q2 = qp_ref[:, (2 * h + 1) * PEER_HALF:(2 * h + 2) * PEER_HALF]
        s1 = _dot_nt(keys_ref[2 * h], q1)
        s2 = _dot_nt(keys_ref[2 * h + 1], q2)
        v1 = _top16_rows(s1)
        v2 = _top16_rows(s2)
        cand = [v1[a] + v2[b] for a, b in _CAND]
        call = jnp.concatenate(cand, axis=0)
        tau = jnp.full_like(cand[0], NEG)
        for c in cand:
            above = jnp.sum((call > c).astype(F32), axis=0, keepdims=True)
            tau = jnp.maximum(tau, jnp.where(above >= PEER_TOPK - 1, c, NEG))
        cmax = cand[0]
        z = jnp.sum(jnp.where(call >= tau, jnp.exp(call - cmax), 0.0), axis=0, keepdims=True)
        s1_ref[h] = s1
        s2_ref[h] = s2
        e1_ref[h] = jnp.exp(s1 - v1[0]) / z
        misc_ref[h] = jnp.concatenate([tau, v2[0], jnp.zeros((6, tau.shape[1]), F32)], axis=0)


def _route(qp, keys, tt):
    T = qp.shape[0]
    big = jax.ShapeDtypeStruct((PEER_HEADS, N_KEYS, T), F32)
    bspec = pl.BlockSpec((PEER_HEADS, N_KEYS, tt), lambda i: (0, 0, i))
    return pl.pallas_call(
        _route_kernel, grid=(T // tt,),
        in_specs=[pl.BlockSpec((tt, qp.shape[1]), lambda i: (i, 0)), _full(keys.shape)],
        out_specs=[bspec, bspec, bspec, pl.BlockSpec((PEER_HEADS, 8, tt), lambda i: (0, 0, i))],
        out_shape=[big, big, big, jax.ShapeDtypeStruct((PEER_HEADS, 8, T), F32)],
        compiler_params=_cparams(("parallel",)), name="peer_route",
    )(qp, keys)


ROWS_PER_STEP = 8
E_TILE = ROWS_PER_STEP * N_KEYS


def _peer_kernel(xn_ref, u_ref, vt_ref, s1_ref, e1_ref, s2_ref, misc_ref, y_ref, e2_sc, acc_sc, at_sc):
    j = pl.program_id(1)

    @pl.when(j == 0)
    def _():
        acc_sc[...] = jnp.zeros(acc_sc.shape, F32)
        for h in range(PEER_HEADS):
            e2_sc[h] = jnp.exp(s2_ref[h] - misc_ref[h, 1:2, :])

    st = _dot_nt(u_ref[...], xn_ref[...])
    for r in range(ROWS_PER_STEP):
        w = None
        for h in range(PEER_HEADS):
            c = s2_ref[h] + s1_ref[h, r:r + 1, :]
            g = jnp.where(c >= misc_ref[h, 0:1, :], e2_sc[h] * e1_ref[h, r:r + 1, :], 0.0)
            w = g if w is None else w + g
        rows = slice(r * N_KEYS, (r + 1) * N_KEYS)
        at_sc[rows, :] = (_gelu(st[rows, :]) * w).astype(BF16)
    acc_sc[...] += _dot(vt_ref[...], at_sc[...])

    @pl.when(j == pl.num_programs(1) - 1)
    def _():
        y_ref[...] = acc_sc[...].T


def _peer_dense(xn, u_tab, vt_tab, s1, e1, s2, misc, tt):
    T = xn.shape[0]
    rows = pl.BlockSpec((PEER_HEADS, ROWS_PER_STEP, tt), lambda i, j: (0, j, i))
    return pl.pallas_call(
        _peer_kernel, grid=(T // tt, N_EXPERTS // E_TILE),
        in_specs=[pl.BlockSpec((tt, D_MODEL), lambda i, j: (i, 0)),
                  pl.BlockSpec((E_TILE, D_MODEL), lambda i, j: (j, 0)),
                  pl.BlockSpec((D_MODEL, E_TILE), lambda i, j: (0, j)),
                  rows, rows,
                  pl.BlockSpec((PEER_HEADS, N_KEYS, tt), lambda i, j: (0, 0, i)),
                  pl.BlockSpec((PEER_HEADS, 8, tt), lambda i, j: (0, 0, i))],
        out_specs=pl.BlockSpec((tt, D_MODEL), lambda i, j: (i, 0)),
        out_shape=jax.ShapeDtypeStruct((T, D_MODEL), F32),
        scratch_shapes=[pltpu.VMEM((PEER_HEADS, N_KEYS, tt), F32), pltpu.VMEM((D_MODEL, tt), F32),
                        pltpu.VMEM((E_TILE, tt), BF16)],
        compiler_params=_cparams(("parallel", "arbitrary")), name="peer_experts",
    )(xn, u_tab, vt_tab, s1, e1, s2, misc)


def _final_kernel(x1_ref, y_ref, p_ref, wple_ref, gple_ref, wgate_ref, o_ref):
    x2 = x1_ref[...] + y_ref[...]
    e = _dot(p_ref[...].astype(BF16), wple_ref[...])
    gate = _sigmoid(_dot(_rms(x2, gple_ref[...], D_MODEL).astype(BF16), wgate_ref[...]))
    o_ref[...] = x2 + e * gate


def _final(x1, y, p, W, tm):
    T = x1.shape[0]
    tok = lambda w: pl.BlockSpec((tm, w), lambda i: (i, 0))
    weights = [W['w_ple'], W['g_ple'], W['w_ple_gate']]
    return pl.pallas_call(
        _final_kernel, grid=(T // tm,),
        in_specs=[tok(D_MODEL), tok(D_MODEL), tok(PLE_DIM)] + [_full(w.shape) for w in weights],
        out_specs=tok(D_MODEL), out_shape=jax.ShapeDtypeStruct((T, D_MODEL), F32),
        compiler_params=_cparams(("parallel",)), name="ple_gate",
    )(x1, y, p, *weights)


def _pad_heads(w, width):
    pad = [(0, 0)] * (w.ndim - 1) + [(0, LANES - width)]
    w = jnp.pad(w, pad)
    return w.reshape(w.shape[:-2] + (HP,))


def _rope_tables(pos):
    inv = ROPE_THETA ** (-jnp.arange(0, QK_ROPE, 2, dtype=F32) / QK_ROPE)
    ang = pos.astype(F32)[:, None] * inv[None, :]
    cos, sin = jnp.cos(ang), jnp.sin(ang)
    T = pos.shape[0]
    cos_t = jnp.concatenate([jnp.ones((T, QK_NOPE), F32), cos, cos, jnp.zeros((T, LANES - QK_DIM), F32)], axis=1)
    sin_t = jnp.concatenate([jnp.zeros((T, QK_NOPE), F32), -sin, sin, jnp.zeros((T, LANES - QK_DIM), F32)], axis=1)
    return cos_t, sin_t


def _swap_rope(w):
    z = jnp.zeros_like(w[..., :QK_NOPE])
    return jnp.concatenate([z, w[..., 80:96], w[..., 64:80]], axis=-1)


def _prep_common(P):
    W = {}
    row = lambda g: g.reshape(1, -1).astype(F32)
    w_in = P['w_in']
    kr = w_in[:, 640:672]
    z64 = jnp.zeros((D_MODEL, QK_NOPE), F32)
    z32 = jnp.zeros((D_MODEL, LANES - QK_DIM), F32)
    W['w_in_p'] = jnp.concatenate([w_in[:, :640], w_in[:, 672:], z64, kr, z32,
                                   z64, kr[:, 16:], kr[:, :16], z32], axis=1).astype(BF16)
    W['g_mix'] = row(P['g_mix'])
    W['g_q_lat'] = row(P['g_q_lat'])
    W['w_uq_p'] = _pad_heads(P['w_uq'], QK_DIM).astype(BF16)
    W['w_uq_sw'] = _pad_heads(_swap_rope(P['w_uq']), QK_DIM).astype(BF16)
    g_qn = P['g_qn']
    W['g_qn_p'] = jnp.pad(g_qn, (0, LANES - QK_DIM)).reshape(1, LANES)
    W['g_qn_sw'] = jnp.pad(_swap_rope(g_qn), (0, LANES - QK_DIM)).reshape(1, LANES)
    lane_head = np.arange(HP) // LANES
    W['ind'] = jnp.asarray(lane_head[:, None] == np.arange(LANES)[None, :], BF16)
    W['expand'] = jnp.asarray(np.arange(LANES)[:, None] == lane_head[None, :], BF16)
    W['g_kv_lat'] = row(P['g_kv_lat'])
    W['w_uk_p'] = _pad_heads(P['w_uk'], QK_NOPE).astype(BF16)
    W['w_uv_p'] = _pad_heads(P['w_uv'], V_DIM).astype(BF16)
    W['w_uk_t'] = jnp.pad(jnp.transpose(P['w_uk'], (1, 2, 0)),
                          ((0, 0), (0, LANES - QK_NOPE), (0, 0))).astype(BF16)
    g_kn = P['g_kn']
    W['g_kn_nope'] = jnp.pad(g_kn[:QK_NOPE], (0, LANES - QK_NOPE)).reshape(1, LANES)
    W['g_kc'] = jnp.pad(g_kn, (0, LANES - QK_DIM)).at[:QK_NOPE].set(0.0).reshape(1, LANES)
    W['g_ks'] = jnp.pad(_swap_rope(g_kn), (0, LANES - QK_DIM)).reshape(1, LANES)

    W['g_attn_p'] = _pad_heads(P['g_attn_out'].reshape(N_HEADS, V_DIM), V_DIM).reshape(1, HP)
    W['g_ssm_out'] = row(P['g_ssm_out'])
    w_out = P['w_out']
    W['w_out_a'] = jnp.pad(w_out[:D_ATTN].reshape(N_HEADS, V_DIM, D_MODEL),
                           ((0, 0), (0, LANES - V_DIM), (0, 0))).reshape(HP, D_MODEL).astype(BF16)
    W['w_out_s'] = w_out[D_ATTN:].astype(BF16)
    W['g_ffn'] = row(P['g_ffn'])
    W['w_peer_q'] = P['w_peer_q'].astype(BF16)
    W['peer_keys'] = P['peer_keys'].reshape(2 * PEER_HEADS, N_KEYS, PEER_HALF).astype(BF16)
    W['peer_u'] = P['peer_u'].astype(BF16)
    W['peer_vt'] = P['peer_v'].T.astype(BF16)
    W['w_ple'] = P['w_ple'].astype(BF16)
    W['g_ple'] = row(P['g_ple'])
    W['w_ple_gate'] = P['w_ple_gate'].astype(BF16)
    return W


def _ffn_tail(attn, ssm, x, p, W, tm, tt_route, tt_peer):
    x1, xn, qp = _mix(attn, ssm, x, W, tm)
    s1, e1, s2, misc = _route(qp, W['peer_keys'], tt_route)
    y = _peer_dense(xn, W['peer_u'], W['peer_vt'], s1, e1, s2, misc, tt_peer)
    return _final(x1, y, p, W, tm)


_PARAM_NAMES = ('g_mix', 'w_in', 'g_q_lat', 'w_uq', 'g_kv_lat', 'w_uk', 'w_uv', 'g_qn', 'g_kn', 'ssm_a_re',
                'ssm_a_im', 'ssm_log_dt', 'ssm_b_re', 'ssm_b_im', 'ssm_c_re', 'ssm_c_im', 'ssm_d', 'w_glu', 'b_glu',
                'g_attn_out', 'g_ssm_out', 'w_out', 'g_ffn', 'w_peer_q', 'peer_keys', 'peer_u', 'peer_v', 'g_ple',
                'w_ple', 'w_ple_gate')

TM_TOKEN = 512
TQ_FLASH = 512
TK_FLASH = 512
TB_SCAN = 512
TT_ROUTE = 256
TT_PEER = 512
PAGES_PER_STEP = 16


def _layer(x_p, x_s, p_p, p_s, caches, states, page_table, P):
    W = _prep_common(P)
    W.update(_prep_ssm(P))
    T = x_p.shape[0]
    B = x_s.shape[0]
    n_past = page_table.shape[1] * PAGE

    cos_t, sin_t = _rope_tables(jnp.arange(T))
    q, k, v, ckv_p, kr_p, ks_p, u = _project(x_p, cos_t, sin_t, W, BF16, TM_TOKEN)
    attn = _flash_attention(q, k, v, TQ_FLASH, TK_FLASH)
    ssm, hre_p, him_p = _s5_prompt(u, W, TB_SCAN)
    y_p = _ffn_tail(attn, ssm, x_p, p_p, W, TM_TOKEN, TT_ROUTE, TT_PEER)

    cos_s, sin_s = _rope_tables(jnp.full((B,), n_past, jnp.int32))
    q, _, _, ckv_s, kr_s, ks_s, u = _project(x_s, cos_s, sin_s, W, F32, B)
    attn = _sample_attention(q, ckv_s, kr_s, ks_s, page_table, *caches, W, PAGES_PER_STEP)
    ssm, hre_s, him_s = _s5_step(u, states[0].reshape(B, N_STATE), states[1].reshape(B, N_STATE), W)
    y_s = _ffn_tail(attn, ssm, x_s, p_s, W, B, B, B)
    st = lambda a, lead: a.reshape(lead + (N_GROUPS, SSM_STATE))
    return (y_p, y_s, (ckv_p, kr_p, ks_p, st(hre_p, (1,)), st(him_p, (1,))),
            (ckv_s, kr_s, ks_s, st(hre_s, (B,)), st(him_s, (B,))))


def kernel(x_prompt, x_sample, cache_kv_latent, cache_k_rope, cache_k_scale, state_ssm_re, state_ssm_im,
           page_table, p_prompt, p_sample, g_mix, w_in, g_q_lat, w_uq, g_kv_lat, w_uk, w_uv, g_qn, g_kn,
           ssm_a_re, ssm_a_im, ssm_log_dt, ssm_b_re, ssm_b_im, ssm_c_re, ssm_c_im, ssm_d, w_glu, b_glu,
           g_attn_out, g_ssm_out, w_out, g_ffn, w_peer_q, peer_keys, peer_u, peer_v, g_ple, w_ple, w_ple_gate):
    params = dict(zip(_PARAM_NAMES, (g_mix, w_in, g_q_lat, w_uq, g_kv_lat, w_uk, w_uv, g_qn, g_kn, ssm_a_re,
                                     ssm_a_im, ssm_log_dt, ssm_b_re, ssm_b_im, ssm_c_re, ssm_c_im, ssm_d, w_glu,
                                     b_glu, g_attn_out, g_ssm_out, w_out, g_ffn, w_peer_q, peer_keys, peer_u,
                                     peer_v, g_ple, w_ple, w_ple_gate)))
    depth = g_mix.shape[0]
    assert x_prompt.shape[0] == 1 and x_sample.shape[1] == 1
    x_p = x_prompt[0]
    x_s = x_sample[:, 0]
    outs_p, outs_s = [], []
    for i in range(depth):
        P = {name: val[i] for name, val in params.items()}
        x_p, x_s, o_p, o_s = _layer(x_p, x_s, p_prompt[i, 0], p_sample[i, :, 0],
                                    (cache_kv_latent[i], cache_k_rope[i], cache_k_scale[i]),
                                    (state_ssm_re[i], state_ssm_im[i]), page_table, P)
        outs_p.append(o_p)
        outs_s.append(o_s)
    stack_p = [jnp.stack([o[j] for o in outs_p])[:, None] for j in range(5)]
    stack_s = [jnp.stack([o[j] for o in outs_s]) for j in range(5)]
    stack_s = [a[:, :, None] if j < 3 else a for j, a in enumerate(stack_s)]
    return (x_p[None], x_s[:, None], stack_p[0], stack_p[1], stack_p[2], stack_p[3][:, 0], stack_p[4][:, 0],
            stack_s[0], stack_s[1], stack_s[2], stack_s[3], stack_s[4])
```

```python
import functools
import math

import jax
import jax.numpy as jnp
import numpy as np
from jax import lax
from jax.experimental import pallas as pl
from jax.experimental.pallas import tpu as pltpu

F32 = jnp.float32
BF16 = jnp.bfloat16

D_MODEL = 1024
N_HEADS = 8
QK_NOPE = 64
QK_ROPE = 32
QK_DIM = QK_NOPE + QK_ROPE
V_DIM = 64
D_ATTN = N_HEADS * V_DIM
Q_RANK = 384
KV_RANK = 256
ROPE_THETA = 10000.0
ATTN_SCALE = QK_DIM ** -0.5
D_SSM = 512
SSM_GROUP = 16
N_GROUPS = 32
SSM_STATE = 64
N_STATE = N_GROUPS * SSM_STATE
PEER_HEADS = 8
N_KEYS = 128
N_EXPERTS = N_KEYS * N_KEYS
PEER_TOPK = 16
PEER_HALF = 128
PLE_DIM = 256
PAGE = 128
EPS = 1e-6

LANES = 128
HP = N_HEADS * LANES
NEG = -1e30
VMEM_LIMIT = 56 * 1024 * 1024


def _cparams(sem, vmem=VMEM_LIMIT):
    return pltpu.CompilerParams(dimension_semantics=sem, vmem_limit_bytes=vmem)


def _dot(a, b):
    return jnp.dot(a, b, preferred_element_type=F32)


def _dot_nt(a, b):
    return lax.dot_general(a, b, (((1,), (1,)), ((), ())), preferred_element_type=F32)


def _dot_hilo(x, w):
    hi = x.astype(BF16)
    lo = (x - hi.astype(F32)).astype(BF16)
    return _dot(hi, w) + _dot(lo, w)


def _dot_hml(x, w):
    hi = x.astype(BF16)
    r1 = x - hi.astype(F32)
    mid = r1.astype(BF16)
    lo = (r1 - mid.astype(F32)).astype(BF16)
    return _dot(hi, w) + _dot(mid, w) + _dot(lo, w)


def _rms(x, g, n):
    ms = jnp.sum(x * x, axis=-1, keepdims=True) * (1.0 / n)
    return x * lax.rsqrt(ms + EPS) * g


def _gelu(x):
    return 0.5 * x * (1.0 + jnp.tanh(0.7978845608028654 * (x + 0.044715 * (x * x * x))))


def _sigmoid(x):
    return 1.0 / (1.0 + jnp.exp(-x))


def _full(shape):
    return pl.BlockSpec(shape, lambda *_: (0,) * len(shape))


def _proj_kernel(x_ref, cos_ref, sin_ref, gmix_ref, win_ref, gq_ref, wuq_ref, wuqs_ref, gqn_ref, gqns_ref,
                 ind_ref, exp_ref, gkv_ref, wuk_ref, wuv_ref, gknn_ref, gkc_ref, gks_ref,
                 q_ref, k_ref, v_ref, ckv_ref, krope_ref, kscale_ref, u_ref):
    x = x_ref[...]
    cos = cos_ref[...]
    sin = sin_ref[...]
    h = _rms(x, gmix_ref[...], D_MODEL).astype(BF16)
    proj = _dot(h, win_ref[...])
    cq = proj[:, :Q_RANK]
    ckv = proj[:, Q_RANK:Q_RANK + KV_RANK]
    u_ref[...] = proj[:, 640:1152]
    krt = proj[:, 1152:1280]
    krts = proj[:, 1280:1408]

    cqn = _rms(cq, gq_ref[...], Q_RANK).astype(BF16)
    q_raw = _dot(cqn, wuq_ref[...])
    q_sw = _dot(cqn, wuqs_ref[...])
    ind = ind_ref[...]
    expd = exp_ref[...]
    ssq = _dot_hilo(q_raw * q_raw, ind)
    r_exp = _dot_hilo(lax.rsqrt(ssq * (1.0 / QK_DIM) + EPS), expd)
    qn = q_raw * r_exp
    qns = q_sw * r_exp
    gqn = gqn_ref[...]
    gqns = gqns_ref[...]

    ckvn = _rms(ckv, gkv_ref[...], KV_RANK)
    ckv_ref[...] = ckvn
    ckvb = ckvn.astype(BF16)
    kn_raw = _dot(ckvb, wuk_ref[...])
    v_ref[...] = _dot(ckvb, wuv_ref[...]).astype(v_ref.dtype)
    ssk = _dot_hilo(kn_raw * kn_raw, ind)
    ms_rope = jnp.sum(krt * krt, axis=-1, keepdims=True)
    ks = lax.rsqrt((ssk + ms_rope) * (1.0 / QK_DIM) + EPS)
    kscale_ref[...] = ks[:, :N_HEADS]
    ks_exp = _dot_hilo(ks, expd)
    kr_tile = (krt * gkc_ref[...]) * cos + (krts * gks_ref[...]) * sin
    krope_ref[...] = kr_tile[:, QK_NOPE:QK_DIM]
    gknn = gknn_ref[...]
    for hd in range(N_HEADS):
        sl = slice(hd * LANES, (hd + 1) * LANES)
        q_ref[:, sl] = ((qn[:, sl] * gqn) * cos + (qns[:, sl] * gqns) * sin).astype(q_ref.dtype)
        k_ref[:, sl] = ((kn_raw[:, sl] * gknn + kr_tile) * ks_exp[:, sl]).astype(k_ref.dtype)


def _project(x, cos_t, sin_t, W, q_dtype, tm):
    T = x.shape[0]
    tok = lambda w: pl.BlockSpec((tm, w), lambda i: (i, 0))
    weights = [W['g_mix'], W['w_in_p'], W['g_q_lat'], W['w_uq_p'], W['w_uq_sw'], W['g_qn_p'], W['g_qn_sw'],
               W['ind'], W['expand'], W['g_kv_lat'], W['w_uk_p'], W['w_uv_p'], W['g_kn_nope'], W['g_kc'], W['g_ks']]
    out_shape = [jax.ShapeDtypeStruct((T, HP), q_dtype), jax.ShapeDtypeStruct((T, HP), BF16),
                 jax.ShapeDtypeStruct((T, HP), BF16), jax.ShapeDtypeStruct((T, KV_RANK), F32),
                 jax.ShapeDtypeStruct((T, QK_ROPE), F32), jax.ShapeDtypeStruct((T, N_HEADS), F32),
                 jax.ShapeDtypeStruct((T, D_SSM), F32)]
    return pl.pallas_call(
        _proj_kernel, grid=(T // tm,),
        in_specs=[tok(D_MODEL), tok(LANES), tok(LANES)] + [_full(w.shape) for w in weights],
        out_specs=[tok(HP), tok(HP), tok(HP), tok(KV_RANK), tok(QK_ROPE), tok(N_HEADS), tok(D_SSM)],
        out_shape=out_shape, compiler_params=_cparams(("parallel",)), name="project",
    )(x, cos_t, sin_t, *weights)


SCORE_SCALE = ATTN_SCALE * math.log2(math.e)


def _flash_kernel(q_ref, k_ref, v_ref, o_ref, m_sc, l_sc, acc_sc, *, tq, tk, ts):
    i = pl.program_id(1)
    nsub = tq // ts
    m_sc[...] = jnp.full(m_sc.shape, NEG, F32)
    l_sc[...] = jnp.zeros(l_sc.shape, F32)
    acc_sc[...] = jnp.zeros(acc_sc.shape, F32)
    lane_tiles = tk // LANES

    def sub_step(a, j, k, v, masked):
        rows = slice(a * ts, (a + 1) * ts)
        s = _dot_nt(q_ref[rows, :], k)
        if masked:
            r = i * tq + a * ts + lax.broadcasted_iota(jnp.int32, (ts, tk), 0)
            c = j * tk + lax.broadcasted_iota(jnp.int32, (ts, tk), 1)
            s = jnp.where(c <= r, s, NEG)
        m_prev = m_sc[rows, :]
        m_new = jnp.maximum(m_prev, jnp.max(s, axis=-1, keepdims=True))
        alpha = jnp.exp2((m_prev - m_new) * SCORE_SCALE)
        p = jnp.exp2((s - jnp.concatenate([m_new] * lane_tiles, axis=1)) * SCORE_SCALE)
        l_sc[rows, :] = alpha * l_sc[rows, :] + jnp.sum(p, axis=-1, keepdims=True)
        acc_sc[rows, :] = alpha * acc_sc[rows, :] + _dot(p.astype(BF16), v)
        m_sc[rows, :] = m_new

    def full_step(j, carry):
        start = pl.multiple_of(j * tk, tk)
        k = k_ref[pl.ds(start, tk), :]
        v = v_ref[pl.ds(start, tk), :]
        for a in range(nsub):
            sub_step(a, j, k, v, False)
        return carry

    n_full = i * (tq // tk)
    lax.fori_loop(0, n_full, full_step, 0)
    for jj in range(tq // tk):
        j = n_full + jj
        start = pl.multiple_of(j * tk, tk)
        k = k_ref[pl.ds(start, tk), :]
        v = v_ref[pl.ds(start, tk), :]
        for a in range(nsub):
            lo_row, hi_row = a * ts, (a + 1) * ts - 1
            lo_col, hi_col = jj * tk, (jj + 1) * tk - 1
            if lo_col > hi_row:
                continue
            sub_step(a, j, k, v, hi_col > lo_row)
    o_ref[...] = acc_sc[...] / l_sc[...]


def _flash_attention(q, k, v, tq, tk, ts):
    T = q.shape[0]
    kern = functools.partial(_flash_kernel, tq=tq, tk=tk, ts=ts)
    return pl.pallas_call(
        kern, grid=(N_HEADS, T // tq),
        in_specs=[pl.BlockSpec((tq, LANES), lambda h, i: (i, h)),
                  pl.BlockSpec((T, LANES), lambda h, i: (0, h)),
                  pl.BlockSpec((T, LANES), lambda h, i: (0, h))],
        out_specs=pl.BlockSpec((tq, LANES), lambda h, i: (i, h)),
        out_shape=jax.ShapeDtypeStruct((T, HP), F32),
        scratch_shapes=[pltpu.VMEM((tq, LANES), F32), pltpu.VMEM((tq, LANES), F32), pltpu.VMEM((tq, LANES), F32)],
        compiler_params=_cparams(("parallel", "arbitrary")), name="flash_attention",
    )(q, k, v)


def _cmul(ar, ai, br, bi):
    return ar * br - ai * bi, ar * bi + ai * br


def _ssm_disc_kernel(are_ref, aim_ref, dt_ref, bre_ref, bim_ref, abr_ref, abi_ref, bbr_ref, bbi_ref):
    lr = are_ref[...]
    li = aim_ref[...]
    dt = dt_ref[...]
    mag = jnp.exp(lr * dt)
    abr = mag * jnp.cos(li * dt)
    abi = mag * jnp.sin(li * dt)
    abr_ref[...] = abr
    abi_ref[...] = abi
    inv = 1.0 / (lr * lr + li * li)
    cr, ci = _cmul(abr - 1.0, abi, lr * inv, -li * inv)
    bbr, bbi = _cmul(cr, ci, bre_ref[...], bim_ref[...])
    bbr_ref[...] = bbr
    bbi_ref[...] = bbi


def _ssm_discretise(P):
    col = lambda a: a.reshape(N_STATE, 1)
    dt = jnp.repeat(jnp.exp(P['ssm_log_dt']), SSM_STATE).reshape(N_STATE, 1)
    bre = P['ssm_b_re'].reshape(N_STATE, SSM_GROUP)
    bim = P['ssm_b_im'].reshape(N_STATE, SSM_GROUP)
    s1 = jax.ShapeDtypeStruct((N_STATE, 1), F32)
    s16 = jax.ShapeDtypeStruct((N_STATE, SSM_GROUP), F32)
    return pl.pallas_call(_ssm_disc_kernel, out_shape=[s1, s1, s16, s16], name="ssm_discretise")(
        col(P['ssm_a_re']), col(P['ssm_a_im']), dt, bre, bim)


def _prep_ssm(P):
    abr, abi, bbr, bbi = _ssm_discretise(P)
    eye = jnp.eye(N_GROUPS, dtype=F32)

    def bdiag_in(b):
        b = b.reshape(N_GROUPS, SSM_STATE, SSM_GROUP)
        return jnp.einsum('gpc,gh->gchp', b, eye).reshape(D_SSM, N_STATE)

    def bdiag_out(c):
        return jnp.einsum('gcp,gh->gphc', c, eye).reshape(N_STATE, D_SSM)

    W = {}
    W['ssm_bm'] = jnp.concatenate([bdiag_in(bbr), bdiag_in(bbi)], axis=1)
    W['ssm_cm'] = jnp.concatenate([bdiag_out(P['ssm_c_re']), -bdiag_out(P['ssm_c_im'])], axis=0)
    W['ssm_ar'] = abr.reshape(1, N_STATE)
    W['ssm_ai'] = abi.reshape(1, N_STATE)
    W['ssm_d'] = P['ssm_d'].reshape(1, D_SSM)
    W['w_glu'] = P['w_glu'].astype(BF16)
    W['b_glu'] = P['b_glu'].reshape(1, D_SSM)
    return W


SCAN_LANES = 512


def _s5_prompt_kernel(u_ref, perm_ref, permt_ref, bm_ref, cm_ref, ar_ref, ai_ref, d_ref, wglu_ref, bglu_ref,
                      out_ref, hre_ref, him_ref, x_sc, carry_sc, *, tb):
    n = tb // 8
    nk = N_STATE // SCAN_LANES
    step = pl.program_id(0)

    @pl.when(step == 0)
    def _():
        carry_sc[...] = jnp.zeros(carry_sc.shape, F32)

    u = u_ref[...]
    up = _dot(perm_ref[...], u.astype(BF16)).astype(BF16)
    for k in range(nk):
        uk = up[:, k * LANES:(k + 1) * LANES]
        for c in range(2):
            lo = c * N_STATE + k * SCAN_LANES
            x_sc[:, lo:lo + SCAN_LANES] = _dot(uk, bm_ref[k * LANES:(k + 1) * LANES, lo:lo + SCAN_LANES])

    for k in range(nk):
        re_sl = slice(k * SCAN_LANES, (k + 1) * SCAN_LANES)
        im_sl = slice(N_STATE + k * SCAN_LANES, N_STATE + (k + 1) * SCAN_LANES)
        ar = jnp.broadcast_to(ar_ref[:, re_sl], (8, SCAN_LANES))
        ai = jnp.broadcast_to(ai_ref[:, re_sl], (8, SCAN_LANES))

        def local_scan(i, h):
            rows = pl.ds(pl.multiple_of(i * 8, 8), 8)
            pr, pi = _cmul(ar, ai, h[0], h[1])
            hr = pr + x_sc[rows, re_sl]
            hi = pi + x_sc[rows, im_sl]
            x_sc[rows, re_sl] = hr
            x_sc[rows, im_sl] = hi
            return hr, hi

        zero = jnp.zeros((8, SCAN_LANES), F32)
        er, ei = lax.fori_loop(0, n, local_scan, (zero, zero))

        pr, pi = ar, ai
        for _ in range(int(math.log2(n))):
            pr, pi = _cmul(pr, pi, pr, pi)
        cr = [carry_sc[0:1, re_sl]]
        ci = [carry_sc[1:2, re_sl]]
        for s in range(1, 9):
            qr, qi = _cmul(pr[0:1], pi[0:1], cr[-1], ci[-1])
            cr.append(qr + er[s - 1:s])
            ci.append(qi + ei[s - 1:s])
        carry_sc[0:1, re_sl] = cr[8]
        carry_sc[1:2, re_sl] = ci[8]
        cin = (jnp.concatenate(cr[:8], axis=0), jnp.concatenate(ci[:8], axis=0))

        def patch(i, c):
            rows = pl.ds(pl.multiple_of(i * 8, 8), 8)
            c = _cmul(ar, ai, c[0], c[1])
            x_sc[rows, re_sl] = x_sc[rows, re_sl] + c[0]
            x_sc[rows, im_sl] = x_sc[rows, im_sl] + c[1]
            return c

        lax.fori_loop(0, n, patch, cin)

    ys = []
    for k in range(nk):
        zr = x_sc[:, k * SCAN_LANES:(k + 1) * SCAN_LANES].astype(BF16)
        zi = x_sc[:, N_STATE + k * SCAN_LANES:N_STATE + (k + 1) * SCAN_LANES].astype(BF16)
        ys.append(_dot(zr, cm_ref[k * SCAN_LANES:(k + 1) * SCAN_LANES, k * LANES:(k + 1) * LANES])
                  + _dot(zi, cm_ref[N_STATE + k * SCAN_LANES:N_STATE + (k + 1) * SCAN_LANES,
                                    k * LANES:(k + 1) * LANES]))
    yp = jnp.concatenate(ys, axis=1)
    pt = permt_ref[...]
    hi = yp.astype(BF16)
    r1 = yp - hi.astype(F32)
    mid = r1.astype(BF16)
    lo = (r1 - mid.astype(F32)).astype(BF16)
    y = _dot(pt, hi) + _dot(pt, mid) + _dot(pt, lo) + d_ref[...] * u
    g = _gelu(y)
    out_ref[...] = g * _sigmoid(_dot(g.astype(BF16), wglu_ref[...]) + bglu_ref[...])
    hre_ref[...] = carry_sc[0:1, :]
    him_ref[...] = carry_sc[1:2, :]


def _s5_prompt(u, W, tb):
    T = u.shape[0]
    n = tb // 8
    r = np.arange(tb)
    src = (r % 8) * n + r // 8
    perm = np.zeros((tb, tb), np.float32)
    perm[r, src] = 1.0
    kern = functools.partial(_s5_prompt_kernel, tb=tb)
    weights = [jnp.asarray(perm, BF16), jnp.asarray(perm.T, BF16), W['ssm_bm'].astype(BF16),
               W['ssm_cm'].astype(BF16), W['ssm_ar'], W['ssm_ai'], W['ssm_d'], W['w_glu'], W['b_glu']]
    return pl.pallas_call(
        kern, grid=(T // tb,),
        in_specs=[pl.BlockSpec((tb, D_SSM), lambda i: (i, 0))] + [_full(w.shape) for w in weights],
        out_specs=[pl.BlockSpec((tb, D_SSM), lambda i: (i, 0)), _full((1, N_STATE)), _full((1, N_STATE))],
        out_shape=[jax.ShapeDtypeStruct((T, D_SSM), F32), jax.ShapeDtypeStruct((1, N_STATE), F32),
                   jax.ShapeDtypeStruct((1, N_STATE), F32)],
        scratch_shapes=[pltpu.VMEM((tb, 2 * N_STATE), F32), pltpu.VMEM((2, N_STATE), F32)],
        compiler_params=_cparams(("arbitrary",)), name="s5_prompt",
    )(u, *weights)


def _s5_step_kernel(u_ref, hre_ref, him_ref, bm_ref, cm_ref, ar_ref, ai_ref, d_ref, wglu_ref, bglu_ref,
                    out_ref, nre_ref, nim_ref):
    u = u_ref[...]
    bm = bm_ref[...]
    uh = u.astype(BF16)
    ul = (u - uh.astype(F32)).astype(BF16)
    bh = bm.astype(BF16)
    bl = (bm - bh.astype(F32)).astype(BF16)
    bu = _dot(uh, bh) + _dot(uh, bl) + _dot(ul, bh)
    pr, pi = _cmul(ar_ref[...], ai_ref[...], hre_ref[...], him_ref[...])
    hr = pr + bu[:, :N_STATE]
    hi = pi + bu[:, N_STATE:]
    nre_ref[...] = hr
    nim_ref[...] = hi
    cm = cm_ref[...]
    y = _dot(hr.astype(BF16), cm[:N_STATE]) + _dot(hi.astype(BF16), cm[N_STATE:]) + d_ref[...] * u
    g = _gelu(y)
    out_ref[...] = g * _sigmoid(_dot(g.astype(BF16), wglu_ref[...]) + bglu_ref[...])


def _s5_step(u, hre, him, W):
    B = u.shape[0]
    st = jax.ShapeDtypeStruct((B, N_STATE), F32)
    return pl.pallas_call(
        _s5_step_kernel, out_shape=[jax.ShapeDtypeStruct((B, D_SSM), F32), st, st],
        compiler_params=pltpu.CompilerParams(vmem_limit_bytes=VMEM_LIMIT), name="s5_step",
    )(u, hre, him, W['ssm_bm'], W['ssm_cm'].astype(BF16), W['ssm_ar'], W['ssm_ai'], W['ssm_d'],
      W['w_glu'], W['b_glu'])


def _qlat_kernel(q_ref, gknn_ref, wukt_ref, o_ref):
    g = gknn_ref[...]
    for h in range(N_HEADS):
        qh = (q_ref[:, h * LANES:(h + 1) * LANES] * g).astype(BF16)
        o_ref[:, h * KV_RANK:(h + 1) * KV_RANK] = _dot(qh, wukt_ref[h]).astype(o_ref.dtype)


def _uv_kernel(ol_ref, wuv_ref, o_ref):
    for h in range(N_HEADS):
        o_ref[:, h * LANES:(h + 1) * LANES] = _dot(ol_ref[:, h * KV_RANK:(h + 1) * KV_RANK].astype(BF16),
                                                   wuv_ref[:, h * LANES:(h + 1) * LANES])


def _decode_kernel(pt_ref, qlat_ref, qr_ref, ckv_ref, krn_ref, ksn_ref, lat_hbm, kr_hbm, ks_hbm, o_ref,
                   lat_buf, kr_buf, ks_buf, sem, m_sc, l_sc, acc_sc, *, pg, nch):
    b = pl.program_id(0)
    c = pl.program_id(1)
    g = b * nch + c
    total = pl.num_programs(0) * nch

    def copies(step, slot, p):
        page = pt_ref[step * pg + p]
        grp = p // PAGES_PER_SEM
        return (pltpu.make_async_copy(lat_hbm.at[page], lat_buf.at[slot, pl.ds(p * PAGE, PAGE)],
                                      sem.at[slot, 0, grp]),
                pltpu.make_async_copy(kr_hbm.at[page], kr_buf.at[slot, p], sem.at[slot, 1, grp]),
                pltpu.make_async_copy(ks_hbm.at[page], ks_buf.at[slot, p], sem.at[slot, 2, grp]))

    def start_all(step, slot):
        def body(p, carry):
            for d in copies(step, slot, p):
                d.start()
            return carry
        lax.fori_loop(0, pg, body, 0)

    def wait_all(step, slot):
        def body(p, carry):
            for d in copies(step, slot, p):
                d.wait()
            return carry
        lax.fori_loop(0, pg, body, 0)

    @pl.when(g == 0)
    def _():
        start_all(0, 0)

    @pl.when(c == 0)
    def _():
        m_sc[...] = jnp.full(m_sc.shape, NEG, F32)
        l_sc[...] = jnp.zeros(l_sc.shape, F32)
        acc_sc[...] = jnp.zeros(acc_sc.shape, F32)

    slot = g % 2
    wait_all(g, slot)

    @pl.when(g + 1 < total)
    def _():
        start_all(g + 1, (g + 1) % 2)

    qlat = qlat_ref[0]
    lat = lat_buf[slot].astype(BF16)
    kr = jnp.concatenate([kr_buf[slot, p] for p in range(pg)], axis=1).astype(BF16)
    ks_t = jnp.concatenate([ks_buf[slot, p] for p in range(pg)], axis=1)
    s = _dot_nt(qlat, lat) + _dot(qr_ref[0], kr)
    s = s * ks_t * ATTN_SCALE
    m_prev = m_sc[...]
    m_new = jnp.maximum(m_prev, jnp.max(s, axis=-1, keepdims=True))
    alpha = jnp.exp(m_prev - m_new)
    p = jnp.exp(s - m_new)
    l_sc[...] = alpha * l_sc[...] + jnp.sum(p, axis=-1, keepdims=True)
    acc_sc[...] = alpha * acc_sc[...] + _dot(p.astype(BF16), lat)
    m_sc[...] = m_new

    @pl.when(c == nch - 1)
    def _():
        ckv = ckv_ref[0]
        s_new = (jnp.sum(qlat.astype(F32) * ckv, axis=-1, keepdims=True)
                 + jnp.sum(qr_ref[0].astype(F32) * krn_ref[0], axis=-1, keepdims=True))
        s_new = s_new * ksn_ref[0] * ATTN_SCALE
        m_prev = m_sc[...]
        m_new = jnp.maximum(m_prev, s_new)
        alpha = jnp.exp(m_prev - m_new)
        p_new = jnp.exp(s_new - m_new)
        l = alpha * l_sc[...] + p_new
        o_ref[0] = (alpha * acc_sc[...] + p_new * ckv) / l


def _decode_attention(page_table, qlat, qr, ckv, krn, ksn, lat_pool, kr_pool, ks_pool, pg):
    B, n_pages = page_table.shape
    nch = n_pages // pg
    tk = pg * PAGE
    kern = functools.partial(_decode_kernel, pg=pg, nch=nch)
    per_b = lambda shape: pl.BlockSpec((1,) + shape, lambda b, c, pt: (b, 0, 0))
    grid_spec = pltpu.PrefetchScalarGridSpec(
        num_scalar_prefetch=1, grid=(B, nch),
        in_specs=[per_b((N_HEADS, KV_RANK)), per_b((N_HEADS, QK_ROPE)), per_b((1, KV_RANK)), per_b((1, QK_ROPE)),
                  per_b((N_HEADS, 1)),
                  pl.BlockSpec(memory_space=pl.ANY), pl.BlockSpec(memory_space=pl.ANY),
                  pl.BlockSpec(memory_space=pl.ANY)],
        out_specs=per_b((N_HEADS, KV_RANK)),
        scratch_shapes=[pltpu.VMEM((2, tk, KV_RANK), F32), pltpu.VMEM((2, pg, QK_ROPE, PAGE), F32),
                        pltpu.VMEM((2, pg, N_HEADS, PAGE), F32),
                        pltpu.SemaphoreType.DMA((2, 3, pl.cdiv(pg, PAGES_PER_SEM))),
                        pltpu.VMEM((N_HEADS, 1), F32), pltpu.VMEM((N_HEADS, 1), F32),
                        pltpu.VMEM((N_HEADS, KV_RANK), F32)])
    return pl.pallas_call(
        kern, grid_spec=grid_spec, out_shape=jax.ShapeDtypeStruct((B, N_HEADS, KV_RANK), F32),
        compiler_params=_cparams(("arbitrary", "arbitrary")), name="decode_attention",
    )(page_table.reshape(-1), qlat, qr, ckv, krn, ksn, lat_pool, kr_pool, ks_pool)


def _sample_attention(q, ckv, krope, kscale, page_table, lat_pool, kr_pool, ks_pool, W, pg):
    B = q.shape[0]
    qlat = pl.pallas_call(_qlat_kernel, out_shape=jax.ShapeDtypeStruct((B, N_HEADS * KV_RANK), BF16),
                          name="q_absorb")(q, W['g_kn_nope'], W['w_uk_t'])
    qr = q.reshape(B, N_HEADS, LANES)[:, :, QK_NOPE:QK_DIM].astype(BF16)
    ol = _decode_attention(page_table, qlat.reshape(B, N_HEADS, KV_RANK), qr, ckv.reshape(B, 1, KV_RANK),
                           krope.reshape(B, 1, QK_ROPE), kscale.reshape(B, N_HEADS, 1),
                           lat_pool, jnp.swapaxes(kr_pool, 1, 2), jnp.swapaxes(ks_pool, 1, 2), pg)
    return pl.pallas_call(_uv_kernel, out_shape=jax.ShapeDtypeStruct((B, HP), F32),
                          name="value_up")(ol.reshape(B, N_HEADS * KV_RANK), W['w_uv_p'])


def _mix_kernel(attn_ref, ssm_ref, x_ref, ga_ref, gs_ref, woa_ref, wos_ref, gffn_ref, wq_ref,
                x1_ref, xn_ref, qp_ref):
    a = _rms(attn_ref[...], ga_ref[...], D_ATTN).astype(BF16)
    s = _rms(ssm_ref[...], gs_ref[...], D_SSM).astype(BF16)
    x1 = x_ref[...] + _dot(a, woa_ref[...]) + _dot(s, wos_ref[...])
    x1_ref[...] = x1
    xn = _rms(x1, gffn_ref[...], D_MODEL).astype(BF16)
    xn_ref[...] = xn
    qp_ref[...] = _dot(xn, wq_ref[...]).astype(BF16)


def _mix(attn, ssm, x, W, tm):
    T = x.shape[0]
    tok = lambda w: pl.BlockSpec((tm, w), lambda i: (i, 0))
    weights = [W['g_attn_p'], W['g_ssm_out'], W['w_out_a'], W['w_out_s'], W['g_ffn'], W['w_peer_q']]
    nq = 2 * PEER_HEADS * PEER_HALF
    return pl.pallas_call(
        _mix_kernel, grid=(T // tm,),
        in_specs=[tok(HP), tok(D_SSM), tok(D_MODEL)] + [_full(w.shape) for w in weights],
        out_specs=[tok(D_MODEL), tok(D_MODEL), tok(nq)],
        out_shape=[jax.ShapeDtypeStruct((T, D_MODEL), F32), jax.ShapeDtypeStruct((T, D_MODEL), BF16),
                   jax.ShapeDtypeStruct((T, nq), BF16)],
        compiler_params=_cparams(("parallel",)), name="mix",
    )(attn, ssm, x, *weights)


_CAND = [(a, b) for a in range(PEER_TOPK) for b in range(PEER_TOPK) if (a + 1) * (b + 1) <= PEER_TOPK]


def _top16_rows(s):
    vals = []
    for _ in range(PEER_TOPK):
        m = jnp.max(s, axis=0, keepdims=True)
        vals.append(m)
        s = jnp.where(s == m, NEG, s)
    return vals


def _top16_ranked(s):
    vals = []
    rank = jnp.full(s.shape, float(PEER_TOPK), F32)
    for k in range(PEER_TOPK):
        m = jnp.max(s, axis=0, keepdims=True)
        vals.append(m)
        hit = s == m
        rank = jnp.where(hit, float(k), rank)
        s = jnp.where(hit, NEG, s)
    return vals, rank


def _route_kernel(qp_ref, keys_ref, n_ref, e1_ref, r2_ref, e2_ref):
    for h in range(PEER_HEADS):
        q1 = qp_ref[:, (2 * h) * PEER_HALF:(2 * h + 1) * PEER_HALF]
        q2 = qp_ref[:, (2 * h + 1) * PEER_HALF:(2 * h + 2) * PEER_HALF]
        s1 = _dot_nt(keys_ref[2 * h], q1)
        s2 = _dot_nt(keys_ref[2 * h + 1], q2)
        v1 = _top16_rows(s1)
        v2, r2 = _top16_ranked(s2)
        cand = [v1[a] + v2[b] for a, b in _CAND]
        call = jnp.concatenate(cand, axis=0)
        tau = _top16_rows(call)[PEER_TOPK - 1]
        cmax = cand[0]
        z = jnp.sum(jnp.where(call >= tau, jnp.exp(call - cmax), 0.0), axis=0, keepdims=True)
        n = jnp.zeros(s1.shape, F32)
        for k2 in range(PEER_TOPK):
            n = n + jnp.where(s1 + v2[k2] >= tau, 1.0, 0.0)
        n_ref[h] = n
        e1_ref[h] = jnp.exp(s1 - v1[0]) / z
        r2_ref[h] = r2.astype(BF16)
        e2_ref[h] = jnp.exp(s2 - v2[0]).astype(BF16)


def _route(qp, keys, tt):
    T = qp.shape[0]
    big = lambda dt: jax.ShapeDtypeStruct((PEER_HEADS, N_KEYS, T), dt)
    bspec = pl.BlockSpec((PEER_HEADS, N_KEYS, tt), lambda i: (0, 0, i))
    return pl.pallas_call(
        _route_kernel, grid=(T // tt,),
        in_specs=[pl.BlockSpec((tt, qp.shape[1]), lambda i: (i, 0)), _full(keys.shape)],
        out_specs=[bspec, bspec, bspec, bspec],
        out_shape=[big(F32), big(F32), big(BF16), big(BF16)],
        compiler_params=_cparams(("parallel",)), name="peer_route",
    )(qp, keys)


ROWS_PER_STEP = 8
E_TILE = ROWS_PER_STEP * N_KEYS


def _peer_kernel(xn_ref, u_ref, vt_ref, n_ref, e1_ref, r2_ref, e2_ref, y_ref, acc_sc, at_sc):
    j = pl.program_id(1)

    @pl.when(j == 0)
    def _():
        acc_sc[...] = jnp.zeros(acc_sc.shape, F32)

    st = _dot_nt(u_ref[...], xn_ref[...])
    zero = jnp.zeros((), BF16)
    for r in range(ROWS_PER_STEP):
        w = None
        for h in range(PEER_HEADS):
            n_row = n_ref[h, r:r + 1, :].astype(BF16)
            e1_row = e1_ref[h, r:r + 1, :].astype(BF16)
            g = jnp.where(r2_ref[h] < n_row, e2_ref[h] * e1_row, zero)
            w = g if w is None else w + g
        rows = slice(r * N_KEYS, (r + 1) * N_KEYS)
        at_sc[rows, :] = _gelu(st[rows, :]).astype(BF16) * w
    acc_sc[...] += _dot(vt_ref[...], at_sc[...])

    @pl.when(j == pl.num_programs(1) - 1)
    def _():
        y_ref[...] = acc_sc[...].T


def _peer_dense(xn, u_tab, vt_tab, n, e1, r2, e2, tt):
    T = xn.shape[0]
    rows = pl.BlockSpec((PEER_HEADS, ROWS_PER_STEP, tt), lambda i, j: (0, j, i))
    keys = pl.BlockSpec((PEER_HEADS, N_KEYS, tt), lambda i, j: (0, 0, i))
    return pl.pallas_call(
        _peer_kernel, grid=(T // tt, N_EXPERTS // E_TILE),
        in_specs=[pl.BlockSpec((tt, D_MODEL), lambda i, j: (i, 0)),
                  pl.BlockSpec((E_TILE, D_MODEL), lambda i, j: (j, 0)),
                  pl.BlockSpec((D_MODEL, E_TILE), lambda i, j: (0, j)),
                  rows, rows, keys, keys],
        out_specs=pl.BlockSpec((tt, D_MODEL), lambda i, j: (i, 0)),
        out_shape=jax.ShapeDtypeStruct((T, D_MODEL), F32),
        scratch_shapes=[pltpu.VMEM((D_MODEL, tt), F32), pltpu.VMEM((E_TILE, tt), BF16)],
        compiler_params=_cparams(("parallel", "arbitrary")), name="peer_experts",
    )(xn, u_tab, vt_tab, n, e1, r2, e2)


def _final_kernel(x1_ref, y_ref, p_ref, wple_ref, gple_ref, wgate_ref, o_ref):
    x2 = x1_ref[...] + y_ref[...]
    e = _dot(p_ref[...].astype(BF16), wple_ref[...])
    gate = _sigmoid(_dot(_rms(x2, gple_ref[...], D_MODEL).astype(BF16), wgate_ref[...]))
    o_ref[...] = x2 + e * gate


def _final(x1, y, p, W, tm):
    T = x1.shape[0]
    tok = lambda w: pl.BlockSpec((tm, w), lambda i: (i, 0))
    weights = [W['w_ple'], W['g_ple'], W['w_ple_gate']]
    return pl.pallas_call(
        _final_kernel, grid=(T // tm,),
        in_specs=[tok(D_MODEL), tok(D_MODEL), tok(PLE_DIM)] + [_full(w.shape) for w in weights],
        out_specs=tok(D_MODEL), out_shape=jax.ShapeDtypeStruct((T, D_MODEL), F32),
        compiler_params=_cparams(("parallel",)), name="ple_gate",
    )(x1, y, p, *weights)


def _pad_heads(w, width):
    pad = [(0, 0)] * (w.ndim - 1) + [(0, LANES - width)]
    w = jnp.pad(w, pad)
    return w.reshape(w.shape[:-2] + (HP,))


def _rope_tables(pos):
    inv = ROPE_THETA ** (-jnp.arange(0, QK_ROPE, 2, dtype=F32) / QK_ROPE)
    ang = pos.astype(F32)[:, None] * inv[None, :]
    cos, sin = jnp.cos(ang), jnp.sin(ang)
    T = pos.shape[0]
    cos_t = jnp.concatenate([jnp.ones((T, QK_NOPE), F32), cos, cos, jnp.zeros((T, LANES - QK_DIM), F32)], axis=1)
    sin_t = jnp.concatenate([jnp.zeros((T, QK_NOPE), F32), -sin, sin, jnp.zeros((T, LANES - QK_DIM), F32)], axis=1)
    return cos_t, sin_t


def _swap_rope(w):
    z = jnp.zeros_like(w[..., :QK_NOPE])
    return jnp.concatenate([z, w[..., 80:96], w[..., 64:80]], axis=-1)


def _prep_common(P):
    W = {}
    row = lambda g: g.reshape(1, -1).astype(F32)
    w_in = P['w_in']
    kr = w_in[:, 640:672]
    z64 = jnp.zeros((D_MODEL, QK_NOPE), F32)
    z32 = jnp.zeros((D_MODEL, LANES - QK_DIM), F32)
    W['w_in_p'] = jnp.concatenate([w_in[:, :640], w_in[:, 672:], z64, kr, z32,
                                   z64, kr[:, 16:], kr[:, :16], z32], axis=1).astype(BF16)
    W['g_mix'] = row(P['g_mix'])
    W['g_q_lat'] = row(P['g_q_lat'])
    W['w_uq_p'] = _pad_heads(P['w_uq'], QK_DIM).astype(BF16)
    W['w_uq_sw'] = _pad_heads(_swap_rope(P['w_uq']), QK_DIM).astype(BF16)
    g_qn = P['g_qn']
    W['g_qn_p'] = jnp.pad(g_qn, (0, LANES - QK_DIM)).reshape(1, LANES)
    W['g_qn_sw'] = jnp.pad(_swap_rope(g_qn), (0, LANES - QK_DIM)).reshape(1, LANES)
    lane_head = np.arange(HP) // LANES
    W['ind'] = jnp.asarray(lane_head[:, None] == np.arange(LANES)[None, :], BF16)
    W['expand'] = jnp.asarray(np.arange(LANES)[:, None] == lane_head[None, :], BF16)
    W['g_kv_lat'] = row(P['g_kv_lat'])
    W['w_uk_p'] = _pad_heads(P['w_uk'], QK_NOPE).astype(BF16)
    W['w_uv_p'] = _pad_heads(P['w_uv'], V_DIM).astype(BF16)
    W['w_uk_t'] = jnp.pad(jnp.transpose(P['w_uk'], (1, 2, 0)),
                          ((0, 0), (0, LANES - QK_NOPE), (0, 0))).astype(BF16)
    g_kn = P['g_kn']
    W['g_kn_nope'] = jnp.pad(g_kn[:QK_NOPE], (0, LANES - QK_NOPE)).reshape(1, LANES)
    W['g_kc'] = jnp.pad(g_kn, (0, LANES - QK_DIM)).at[:QK_NOPE].set(0.0).reshape(1, LANES)
    W['g_ks'] = jnp.pad(_swap_rope(g_kn), (0, LANES - QK_DIM)).reshape(1, LANES)

    W['g_attn_p'] = _pad_heads(P['g_attn_out'].reshape(N_HEADS, V_DIM), V_DIM).reshape(1, HP)
    W['g_ssm_out'] = row(P['g_ssm_out'])
    w_out = P['w_out']
    W['w_out_a'] = jnp.pad(w_out[:D_ATTN].reshape(N_HEADS, V_DIM, D_MODEL),
                           ((0, 0), (0, LANES - V_DIM), (0, 0))).reshape(HP, D_MODEL).astype(BF16)
    W['w_out_s'] = w_out[D_ATTN:].astype(BF16)
    W['g_ffn'] = row(P['g_ffn'])
    W['w_peer_q'] = P['w_peer_q'].astype(BF16)
    W['peer_keys'] = P['peer_keys'].reshape(2 * PEER_HEADS, N_KEYS, PEER_HALF).astype(BF16)
    W['peer_u'] = P['peer_u'].astype(BF16)
    W['peer_vt'] = P['peer_v'].T.astype(BF16)
    W['w_ple'] = P['w_ple'].astype(BF16)
    W['g_ple'] = row(P['g_ple'])
    W['w_ple_gate'] = P['w_ple_gate'].astype(BF16)
    return W


def _ffn_tail(attn, ssm, x, p, W, tm, tt_route, tt_peer):
    x1, xn, qp = _mix(attn, ssm, x, W, tm)
    n, e1, r2, e2 = _route(qp, W['peer_keys'], tt_route)
    y = _peer_dense(xn, W['peer_u'], W['peer_vt'], n, e1, r2, e2, tt_peer)
    return _final(x1, y, p, W, tm)


_PARAM_NAMES = ('g_mix', 'w_in', 'g_q_lat', 'w_uq', 'g_kv_lat', 'w_uk', 'w_uv', 'g_qn', 'g_kn', 'ssm_a_re',
                'ssm_a_im', 'ssm_log_dt', 'ssm_b_re', 'ssm_b_im', 'ssm_c_re', 'ssm_c_im', 'ssm_d', 'w_glu', 'b_glu',
                'g_attn_out', 'g_ssm_out', 'w_out', 'g_ffn', 'w_peer_q', 'peer_keys', 'peer_u', 'peer_v', 'g_ple',
                'w_ple', 'w_ple_gate')

TM_TOKEN = 512
TQ_FLASH = 1024
TK_FLASH = 512
TS_FLASH = 512
TB_SCAN = 512
TT_ROUTE = 256
TT_PEER = 512
PAGES_PER_STEP = 32
PAGES_PER_SEM = 16


def _layer(x_p, x_s, p_p, p_s, caches, states, page_table, P):
    W = _prep_common(P)
    W.update(_prep_ssm(P))
    T = x_p.shape[0]
    B = x_s.shape[0]
    n_past = page_table.shape[1] * PAGE

    cos_t, sin_t = _rope_tables(jnp.arange(T))
    q, k, v, ckv_p, kr_p, ks_p, u = _project(x_p, cos_t, sin_t, W, BF16, TM_TOKEN)
    attn = _flash_attention(q, k, v, TQ_FLASH, TK_FLASH, TS_FLASH)
    ssm, hre_p, him_p = _s5_prompt(u, W, TB_SCAN)
    y_p = _ffn_tail(attn, ssm, x_p, p_p, W, TM_TOKEN, TT_ROUTE, TT_PEER)

    cos_s, sin_s = _rope_tables(jnp.full((B,), n_past, jnp.int32))
    q, _, _, ckv_s, kr_s, ks_s, u = _project(x_s, cos_s, sin_s, W, F32, B)
    attn = _sample_attention(q, ckv_s, kr_s, ks_s, page_table, *caches, W, PAGES_PER_STEP)
    ssm, hre_s, him_s = _s5_step(u, states[0].reshape(B, N_STATE), states[1].reshape(B, N_STATE), W)
    y_s = _ffn_tail(attn, ssm, x_s, p_s, W, B, B, B)
    st = lambda a, lead: a.reshape(lead + (N_GROUPS, SSM_STATE))
    return (y_p, y_s, (ckv_p, kr_p, ks_p, st(hre_p, (1,)), st(him_p, (1,))),
            (ckv_s, kr_s, ks_s, st(hre_s, (B,)), st(him_s, (B,))))


def kernel(x_prompt, x_sample, cache_kv_latent, cache_k_rope, cache_k_scale, state_ssm_re, state_ssm_im,
           page_table, p_prompt, p_sample, g_mix, w_in, g_q_lat, w_uq, g_kv_lat, w_uk, w_uv, g_qn, g_kn,
           ssm_a_re, ssm_a_im, ssm_log_dt, ssm_b_re, ssm_b_im, ssm_c_re, ssm_c_im, ssm_d, w_glu, b_glu,
           g_attn_out, g_ssm_out, w_out, g_ffn, w_peer_q, peer_keys, peer_u, peer_v, g_ple, w_ple, w_ple_gate):
    params = dict(zip(_PARAM_NAMES, (g_mix, w_in, g_q_lat, w_uq, g_kv_lat, w_uk, w_uv, g_qn, g_kn, ssm_a_re,
                                     ssm_a_im, ssm_log_dt, ssm_b_re, ssm_b_im, ssm_c_re, ssm_c_im, ssm_d, w_glu,
                                     b_glu, g_attn_out, g_ssm_out, w_out, g_ffn, w_peer_q, peer_keys, peer_u,
                                     peer_v, g_ple, w_ple, w_ple_gate)))
    depth = g_mix.shape[0]
    assert x_prompt.shape[0] == 1 and x_sample.shape[1] == 1
    x_p = x_prompt[0]
    x_s = x_sample[:, 0]
    outs_p, outs_s = [], []
    for i in range(depth):
        P = {name: val[i] for name, val in params.items()}
        x_p, x_s, o_p, o_s = _layer(x_p, x_s, p_prompt[i, 0], p_sample[i, :, 0],
                                    (cache_kv_latent[i], cache_k_rope[i], cache_k_scale[i]),
                                    (state_ssm_re[i], state_ssm_im[i]), page_table, P)
        outs_p.append(o_p)
        outs_s.append(o_s)
    stack_p = [jnp.stack([o[j] for o in outs_p])[:, None] for j in range(5)]
    stack_s = [jnp.stack([o[j] for o in outs_s]) for j in range(5)]
    stack_s = [a[:, :, None] if j < 3 else a for j, a in enumerate(stack_s)]
    return (x_p[None], x_s[:, None], stack_p[0], stack_p[1], stack_p[2], stack_p[3][:, 0], stack_p[4][:, 0],
            stack_s[0], stack_s[1], stack_s[2], stack_s[3], stack_s[4])
```

```python
import functools
import math

import jax
import jax.numpy as jnp
import numpy as np
from jax import lax
from jax.experimental import pallas as pl
from jax.experimental.pallas import tpu as pltpu

F32 = jnp.float32
BF16 = jnp.bfloat16

D_MODEL = 1024
N_HEADS = 8
QK_NOPE = 64
QK_ROPE = 32
QK_DIM = QK_NOPE + QK_ROPE
V_DIM = 64
D_ATTN = N_HEADS * V_DIM
Q_RANK = 384
KV_RANK = 256
ROPE_THETA = 10000.0
ATTN_SCALE = QK_DIM ** -0.5
D_SSM = 512
SSM_GROUP = 16
N_GROUPS = 32
SSM_STATE = 64
N_STATE = N_GROUPS * SSM_STATE
PEER_HEADS = 8
N_KEYS = 128
N_EXPERTS = N_KEYS * N_KEYS
PEER_TOPK = 16
PEER_HALF = 128
PLE_DIM = 256
PAGE = 128
EPS = 1e-6

LANES = 128
BF16_ROWS = 16
HP = N_HEADS * LANES
NEG = -1e30
VMEM_LIMIT = 56 * 1024 * 1024


def _cparams(sem, vmem=VMEM_LIMIT):
    return pltpu.CompilerParams(dimension_semantics=sem, vmem_limit_bytes=vmem)


def _dot(a, b):
    return jnp.dot(a, b, preferred_element_type=F32)


def _dot_nt(a, b):
    return lax.dot_general(a, b, (((1,), (1,)), ((), ())), preferred_element_type=F32)


def _dot_hilo(x, w):
    hi = x.astype(BF16)
    lo = (x - hi.astype(F32)).astype(BF16)
    return _dot(hi, w) + _dot(lo, w)


def _rms(x, g, n):
    ms = jnp.sum(x * x, axis=-1, keepdims=True) * (1.0 / n)
    return x * lax.rsqrt(ms + EPS) * g


def _gelu(x):
    return 0.5 * x * (1.0 + jnp.tanh(0.7978845608028654 * (x + 0.044715 * (x * x * x))))


def _gelu_x2(x):
    c0 = 0.7978845608028654
    t = jnp.tanh(x * (c0 + (c0 * 0.044715) * (x * x)))
    return x + x * t


def _sigmoid(x):
    return 1.0 / (1.0 + jnp.exp(-x))


def _full(shape):
    return pl.BlockSpec(shape, lambda *_: (0,) * len(shape))


def _proj_kernel(x_ref, cos_ref, sin_ref, gmix_ref, win_ref, gq_ref, wuq_ref, wuqs_ref, gqn_ref, gqns_ref,
                 ind_ref, exp_ref, gkv_ref, wuk_ref, wuv_ref, gknn_ref, gkc_ref, gks_ref,
                 q_ref, k_ref, v_ref, ckv_ref, krope_ref, kscale_ref, u_ref):
    x = x_ref[...]
    cos = cos_ref[...]
    sin = sin_ref[...]
    h = _rms(x, gmix_ref[...], D_MODEL).astype(BF16)
    proj = _dot(h, win_ref[...])
    cq = proj[:, :Q_RANK]
    ckv = proj[:, Q_RANK:Q_RANK + KV_RANK]
    u_ref[...] = proj[:, 640:1152]
    krt = proj[:, 1152:1280]
    krts = proj[:, 1280:1408]

    cqn = _rms(cq, gq_ref[...], Q_RANK).astype(BF16)
    q_raw = _dot(cqn, wuq_ref[...])
    q_sw = _dot(cqn, wuqs_ref[...])
    ind = ind_ref[...]
    expd = exp_ref[...]
    ssq = _dot_hilo(q_raw * q_raw, ind)
    r_exp = _dot_hilo(lax.rsqrt(ssq * (1.0 / QK_DIM) + EPS), expd)
    qn = q_raw * r_exp
    qns = q_sw * r_exp
    gqn = gqn_ref[...]
    gqns = gqns_ref[...]

    ckvn = _rms(ckv, gkv_ref[...], KV_RANK)
    ckv_ref[...] = ckvn
    ckvb = ckvn.astype(BF16)
    kn_raw = _dot(ckvb, wuk_ref[...])
    v_ref[...] = _dot(ckvb, wuv_ref[...]).astype(v_ref.dtype)
    ssk = _dot_hilo(kn_raw * kn_raw, ind)
    ms_rope = jnp.sum(krt * krt, axis=-1, keepdims=True)
    ks = lax.rsqrt((ssk + ms_rope) * (1.0 / QK_DIM) + EPS)
    kscale_ref[...] = ks[:, :N_HEADS]
    ks_exp = _dot_hilo(ks, expd)
    kr_tile = (krt * gkc_ref[...]) * cos + (krts * gks_ref[...]) * sin
    krope_ref[...] = kr_tile[:, QK_NOPE:QK_DIM]
    gknn = gknn_ref[...]
    for hd in range(N_HEADS):
        sl = slice(hd * LANES, (hd + 1) * LANES)
        q_ref[:, sl] = ((qn[:, sl] * gqn) * cos + (qns[:, sl] * gqns) * sin).astype(q_ref.dtype)
        k_ref[:, sl] = ((kn_raw[:, sl] * gknn + kr_tile) * ks_exp[:, sl]).astype(k_ref.dtype)


def _project(x, cos_t, sin_t, W, q_dtype, tm):
    T = x.shape[0]
    tok = lambda w: pl.BlockSpec((tm, w), lambda i: (i, 0))
    weights = [W['g_mix'], W['w_in_p'], W['g_q_lat'], W['w_uq_p'], W['w_uq_sw'], W['g_qn_p'], W['g_qn_sw'],
               W['ind'], W['expand'], W['g_kv_lat'], W['w_uk_p'], W['w_uv_p'], W['g_kn_nope'], W['g_kc'], W['g_ks']]
    out_shape = [jax.ShapeDtypeStruct((T, HP), q_dtype), jax.ShapeDtypeStruct((T, HP), BF16),
                 jax.ShapeDtypeStruct((T, HP), BF16), jax.ShapeDtypeStruct((T, KV_RANK), F32),
                 jax.ShapeDtypeStruct((T, QK_ROPE), F32), jax.ShapeDtypeStruct((T, N_HEADS), F32),
                 jax.ShapeDtypeStruct((T, D_SSM), F32)]
    return pl.pallas_call(
        _proj_kernel, grid=(T // tm,),
        in_specs=[tok(D_MODEL), tok(LANES), tok(LANES)] + [_full(w.shape) for w in weights],
        out_specs=[tok(HP), tok(HP), tok(HP), tok(KV_RANK), tok(QK_ROPE), tok(N_HEADS), tok(D_SSM)],
        out_shape=out_shape, compiler_params=_cparams(("parallel",)), name="project",
    )(x, cos_t, sin_t, *weights)


SCORE_SCALE = ATTN_SCALE * math.log2(math.e)


def _flash_kernel(q_ref, k_ref, v_ref, o_ref, m_sc, l_sc, acc_sc, *, tq, tk, ts):
    i = pl.program_id(1)
    nsub = tq // ts
    m_sc[...] = jnp.full(m_sc.shape, NEG, F32)
    l_sc[...] = jnp.zeros(l_sc.shape, F32)
    acc_sc[...] = jnp.zeros(acc_sc.shape, F32)
    lane_tiles = tk // LANES

    def sub_step(a, j, k, v, masked):
        rows = slice(a * ts, (a + 1) * ts)
        s = _dot_nt(q_ref[rows, :], k)
        if masked:
            r = i * tq + a * ts + lax.broadcasted_iota(jnp.int32, (ts, tk), 0)
            c = j * tk + lax.broadcasted_iota(jnp.int32, (ts, tk), 1)
            s = jnp.where(c <= r, s, NEG)
        m_prev = m_sc[rows, :]
        m_new = jnp.maximum(m_prev, jnp.max(s, axis=-1, keepdims=True))
        alpha = jnp.exp2((m_prev - m_new) * SCORE_SCALE)
        p = jnp.exp2((s - jnp.concatenate([m_new] * lane_tiles, axis=1)) * SCORE_SCALE)
        l_sc[rows, :] = alpha * l_sc[rows, :] + jnp.sum(p, axis=-1, keepdims=True)
        acc_sc[rows, :] = alpha * acc_sc[rows, :] + _dot(p.astype(BF16), v)
        m_sc[rows, :] = m_new

    def full_step(j, carry):
        start = pl.multiple_of(j * tk, tk)
        k = k_ref[pl.ds(start, tk), :]
        v = v_ref[pl.ds(start, tk), :]
        for a in range(nsub):
            sub_step(a, j, k, v, False)
        return carry

    n_full = i * (tq // tk)
    lax.fori_loop(0, n_full, full_step, 0)
    for jj in range(tq // tk):
        j = n_full + jj
        start = pl.multiple_of(j * tk, tk)
        k = k_ref[pl.ds(start, tk), :]
        v = v_ref[pl.ds(start, tk), :]
        for a in range(nsub):
            lo_row, hi_row = a * ts, (a + 1) * ts - 1
            lo_col, hi_col = jj * tk, (jj + 1) * tk - 1
            if lo_col > hi_row:
                continue
            sub_step(a, j, k, v, hi_col > lo_row)
    o_ref[...] = acc_sc[...] / l_sc[...]


def _flash_attention(q, k, v, tq, tk, ts):
    T = q.shape[0]
    kern = functools.partial(_flash_kernel, tq=tq, tk=tk, ts=ts)
    return pl.pallas_call(
        kern, grid=(N_HEADS, T // tq),
        in_specs=[pl.BlockSpec((tq, LANES), lambda h, i: (i, h)),
                  pl.BlockSpec((T, LANES), lambda h, i: (0, h)),
                  pl.BlockSpec((T, LANES), lambda h, i: (0, h))],
        out_specs=pl.BlockSpec((tq, LANES), lambda h, i: (i, h)),
        out_shape=jax.ShapeDtypeStruct((T, HP), F32),
        scratch_shapes=[pltpu.VMEM((tq, LANES), F32), pltpu.VMEM((tq, LANES), F32), pltpu.VMEM((tq, LANES), F32)],
        compiler_params=_cparams(("parallel", "arbitrary")), name="flash_attention",
    )(q, k, v)


def _cmul(ar, ai, br, bi):
    return ar * br - ai * bi, ar * bi + ai * br


def _ssm_disc_kernel(are_ref, aim_ref, dt_ref, bre_ref, bim_ref, abr_ref, abi_ref, bbr_ref, bbi_ref):
    lr = are_ref[...]
    li = aim_ref[...]
    dt = dt_ref[...]
    mag = jnp.exp(lr * dt)
    abr = mag * jnp.cos(li * dt)
    abi = mag * jnp.sin(li * dt)
    abr_ref[...] = abr
    abi_ref[...] = abi
    inv = 1.0 / (lr * lr + li * li)
    cr, ci = _cmul(abr - 1.0, abi, lr * inv, -li * inv)
    bbr, bbi = _cmul(cr, ci, bre_ref[...], bim_ref[...])
    bbr_ref[...] = bbr
    bbi_ref[...] = bbi


def _ssm_discretise(P):
    col = lambda a: a.reshape(N_STATE, 1)
    dt = jnp.repeat(jnp.exp(P['ssm_log_dt']), SSM_STATE).reshape(N_STATE, 1)
    bre = P['ssm_b_re'].reshape(N_STATE, SSM_GROUP)
    bim = P['ssm_b_im'].reshape(N_STATE, SSM_GROUP)
    s1 = jax.ShapeDtypeStruct((N_STATE, 1), F32)
    s16 = jax.ShapeDtypeStruct((N_STATE, SSM_GROUP), F32)
    return pl.pallas_call(_ssm_disc_kernel, out_shape=[s1, s1, s16, s16], name="ssm_discretise")(
        col(P['ssm_a_re']), col(P['ssm_a_im']), dt, bre, bim)


def _prep_ssm(P):
    abr, abi, bbr, bbi = _ssm_discretise(P)
    eye = jnp.eye(N_GROUPS, dtype=F32)

    def bdiag_in(b):
        b = b.reshape(N_GROUPS, SSM_STATE, SSM_GROUP)
        return jnp.einsum('gpc,gh->gchp', b, eye).reshape(D_SSM, N_STATE)

    def bdiag_out(c):
        return jnp.einsum('gcp,gh->gphc', c, eye).reshape(N_STATE, D_SSM)

    W = {}
    W['ssm_bm'] = jnp.concatenate([bdiag_in(bbr), bdiag_in(bbi)], axis=1)
    W['ssm_cm'] = jnp.concatenate([bdiag_out(P['ssm_c_re']), -bdiag_out(P['ssm_c_im'])], axis=0)
    W['ssm_ar'] = abr.reshape(1, N_STATE)
    W['ssm_ai'] = abi.reshape(1, N_STATE)
    W['ssm_d'] = P['ssm_d'].reshape(1, D_SSM)
    W['w_glu'] = P['w_glu'].astype(BF16)
    W['b_glu'] = P['b_glu'].reshape(1, D_SSM)
    return W


SCAN_LANES = 512


def _s5_prompt_kernel(u_ref, perm_ref, permt_ref, bm_ref, cm_ref, ar_ref, ai_ref, d_ref, wglu_ref, bglu_ref,
                      out_ref, hre_ref, him_ref, x_sc, carry_sc, *, tb):
    n = tb // 8
    nk = N_STATE // SCAN_LANES
    step = pl.program_id(0)

    @pl.when(step == 0)
    def _():
        carry_sc[...] = jnp.zeros(carry_sc.shape, F32)

    u = u_ref[...]
    up = _dot(perm_ref[...], u.astype(BF16)).astype(BF16)
    for k in range(nk):
        uk = up[:, k * LANES:(k + 1) * LANES]
        for c in range(2):
            lo = c * N_STATE + k * SCAN_LANES
            x_sc[:, lo:lo + SCAN_LANES] = _dot(uk, bm_ref[k * LANES:(k + 1) * LANES, lo:lo + SCAN_LANES])

    for k in range(nk):
        re_sl = slice(k * SCAN_LANES, (k + 1) * SCAN_LANES)
        im_sl = slice(N_STATE + k * SCAN_LANES, N_STATE + (k + 1) * SCAN_LANES)
        ar = jnp.broadcast_to(ar_ref[:, re_sl], (8, SCAN_LANES))
        ai = jnp.broadcast_to(ai_ref[:, re_sl], (8, SCAN_LANES))

        def local_scan(i, h):
            rows = pl.ds(pl.multiple_of(i * 8, 8), 8)
            pr, pi = _cmul(ar, ai, h[0], h[1])
            hr = pr + x_sc[rows, re_sl]
            hi = pi + x_sc[rows, im_sl]
            x_sc[rows, re_sl] = hr
            x_sc[rows, im_sl] = hi
            return hr, hi

        zero = jnp.zeros((8, SCAN_LANES), F32)
        er, ei = lax.fori_loop(0, n, local_scan, (zero, zero))

        pr, pi = ar, ai
        for _ in range(int(math.log2(n))):
            pr, pi = _cmul(pr, pi, pr, pi)
        cr = [carry_sc[0:1, re_sl]]
        ci = [carry_sc[1:2, re_sl]]
        for s in range(1, 9):
            qr, qi = _cmul(pr[0:1], pi[0:1], cr[-1], ci[-1])
            cr.append(qr + er[s - 1:s])
            ci.append(qi + ei[s - 1:s])
        carry_sc[0:1, re_sl] = cr[8]
        carry_sc[1:2, re_sl] = ci[8]
        cin = (jnp.concatenate(cr[:8], axis=0), jnp.concatenate(ci[:8], axis=0))

        def patch(i, c):
            rows = pl.ds(pl.multiple_of(i * 8, 8), 8)
            c = _cmul(ar, ai, c[0], c[1])
            x_sc[rows, re_sl] = x_sc[rows, re_sl] + c[0]
            x_sc[rows, im_sl] = x_sc[rows, im_sl] + c[1]
            return c

        lax.fori_loop(0, n, patch, cin)

    ys = []
    for k in range(nk):
        zr = x_sc[:, k * SCAN_LANES:(k + 1) * SCAN_LANES].astype(BF16)
        zi = x_sc[:, N_STATE + k * SCAN_LANES:N_STATE + (k + 1) * SCAN_LANES].astype(BF16)
        ys.append(_dot(zr, cm_ref[k * SCAN_LANES:(k + 1) * SCAN_LANES, k * LANES:(k + 1) * LANES])
                  + _dot(zi, cm_ref[N_STATE + k * SCAN_LANES:N_STATE + (k + 1) * SCAN_LANES,
                                    k * LANES:(k + 1) * LANES]))
    yp = jnp.concatenate(ys, axis=1)
    pt = permt_ref[...]
    hi = yp.astype(BF16)
    r1 = yp - hi.astype(F32)
    mid = r1.astype(BF16)
    lo = (r1 - mid.astype(F32)).astype(BF16)
    y = _dot(pt, hi) + _dot(pt, mid) + _dot(pt, lo) + d_ref[...] * u
    g = _gelu(y)
    out_ref[...] = g * _sigmoid(_dot(g.astype(BF16), wglu_ref[...]) + bglu_ref[...])
    hre_ref[...] = carry_sc[0:1, :]
    him_ref[...] = carry_sc[1:2, :]


def _s5_prompt(u, W, tb):
    T = u.shape[0]
    n = tb // 8
    r = np.arange(tb)
    src = (r % 8) * n + r // 8
    perm = np.zeros((tb, tb), np.float32)
    perm[r, src] = 1.0
    kern = functools.partial(_s5_prompt_kernel, tb=tb)
    weights = [jnp.asarray(perm, BF16), jnp.asarray(perm.T, BF16), W['ssm_bm'].astype(BF16),
               W['ssm_cm'].astype(BF16), W['ssm_ar'], W['ssm_ai'], W['ssm_d'], W['w_glu'], W['b_glu']]
    return pl.pallas_call(
        kern, grid=(T // tb,),
        in_specs=[pl.BlockSpec((tb, D_SSM), lambda i: (i, 0))] + [_full(w.shape) for w in weights],
        out_specs=[pl.BlockSpec((tb, D_SSM), lambda i: (i, 0)), _full((1, N_STATE)), _full((1, N_STATE))],
        out_shape=[jax.ShapeDtypeStruct((T, D_SSM), F32), jax.ShapeDtypeStruct((1, N_STATE), F32),
                   jax.ShapeDtypeStruct((1, N_STATE), F32)],
        scratch_shapes=[pltpu.VMEM((tb, 2 * N_STATE), F32), pltpu.VMEM((2, N_STATE), F32)],
        compiler_params=_cparams(("arbitrary",)), name="s5_prompt",
    )(u, *weights)


def _s5_step_kernel(u_ref, hre_ref, him_ref, bm_ref, cm_ref, ar_ref, ai_ref, d_ref, wglu_ref, bglu_ref,
                    out_ref, nre_ref, nim_ref):
    u = u_ref[...]
    bm = bm_ref[...]
    uh = u.astype(BF16)
    ul = (u - uh.astype(F32)).astype(BF16)
    bh = bm.astype(BF16)
    bl = (bm - bh.astype(F32)).astype(BF16)
    bu = _dot(uh, bh) + _dot(uh, bl) + _dot(ul, bh)
    pr, pi = _cmul(ar_ref[...], ai_ref[...], hre_ref[...], him_ref[...])
    hr = pr + bu[:, :N_STATE]
    hi = pi + bu[:, N_STATE:]
    nre_ref[...] = hr
    nim_ref[...] = hi
    cm = cm_ref[...]
    y = _dot(hr.astype(BF16), cm[:N_STATE]) + _dot(hi.astype(BF16), cm[N_STATE:]) + d_ref[...] * u
    g = _gelu(y)
    out_ref[...] = g * _sigmoid(_dot(g.astype(BF16), wglu_ref[...]) + bglu_ref[...])


def _s5_step(u, hre, him, W):
    B = u.shape[0]
    st = jax.ShapeDtypeStruct((B, N_STATE), F32)
    return pl.pallas_call(
        _s5_step_kernel, out_shape=[jax.ShapeDtypeStruct((B, D_SSM), F32), st, st],
        compiler_params=pltpu.CompilerParams(vmem_limit_bytes=VMEM_LIMIT), name="s5_step",
    )(u, hre, him, W['ssm_bm'], W['ssm_cm'].astype(BF16), W['ssm_ar'], W['ssm_ai'], W['ssm_d'],
      W['w_glu'], W['b_glu'])


def _qlat_kernel(q_ref, gknn_ref, wukt_ref, o_ref):
    g = gknn_ref[...]
    for h in range(N_HEADS):
        qh = (q_ref[:, h * LANES:(h + 1) * LANES] * g).astype(BF16)
        o_ref[:, h * KV_RANK:(h + 1) * KV_RANK] = _dot(qh, wukt_ref[h]).astype(o_ref.dtype)


def _uv_kernel(ol_ref, wuv_ref, o_ref):
    for h in range(N_HEADS):
        o_ref[:, h * LANES:(h + 1) * LANES] = _dot(ol_ref[:, h * KV_RANK:(h + 1) * KV_RANK].astype(BF16),
                                                   wuv_ref[:, h * LANES:(h + 1) * LANES])


def _decode_kernel(pt_ref, qlat_ref, qr_ref, ckv_ref, krn_ref, ksn_ref, lat_hbm, kr_hbm, ks_hbm, o_ref,
                   lat_buf, kr_buf, ks_buf, sem, m_sc, l_sc, acc_sc, *, pg, nch):
    b = pl.program_id(0)
    c = pl.program_id(1)
    g = b * nch + c
    total = pl.num_programs(0) * nch

    def copies(step, slot, p):
        page = pt_ref[step * pg + p]
        grp = p // PAGES_PER_SEM
        return (pltpu.make_async_copy(lat_hbm.at[page], lat_buf.at[slot, pl.ds(p * PAGE, PAGE)],
                                      sem.at[slot, 0, grp]),
                pltpu.make_async_copy(kr_hbm.at[page], kr_buf.at[slot, p], sem.at[slot, 1, grp]),
                pltpu.make_async_copy(ks_hbm.at[page], ks_buf.at[slot, p], sem.at[slot, 2, grp]))

    def start_all(step, slot):
        def body(p, carry):
            for d in copies(step, slot, p):
                d.start()
            return carry
        lax.fori_loop(0, pg, body, 0, unroll=DMA_ISSUE_UNROLL)

    def wait_all(slot):
        for grp in range(pl.cdiv(pg, PAGES_PER_SEM)):
            first = grp * PAGES_PER_SEM
            count = min(PAGES_PER_SEM, pg - first)
            lat_grp = lat_buf.at[slot, pl.ds(first * PAGE, count * PAGE)]
            kr_grp = kr_buf.at[slot, pl.ds(first, count)]
            ks_grp = ks_buf.at[slot, pl.ds(first, count)]
            pltpu.make_async_copy(lat_grp, lat_grp, sem.at[slot, 0, grp]).wait()
            pltpu.make_async_copy(kr_grp, kr_grp, sem.at[slot, 1, grp]).wait()
            pltpu.make_async_copy(ks_grp, ks_grp, sem.at[slot, 2, grp]).wait()

    @pl.when(g == 0)
    def _():
        start_all(0, 0)

    @pl.when(c == 0)
    def _():
        m_sc[...] = jnp.full(m_sc.shape, NEG, F32)
        l_sc[...] = jnp.zeros(l_sc.shape, F32)
        acc_sc[...] = jnp.zeros(acc_sc.shape, F32)

    slot = g % 2
    wait_all(slot)

    @pl.when(g + 1 < total)
    def _():
        start_all(g + 1, (g + 1) % 2)

    qlat = qlat_ref[0]
    lat = lat_buf[slot].astype(BF16)
    kr = jnp.concatenate([kr_buf[slot, p] for p in range(pg)], axis=1).astype(BF16)
    ks_t = jnp.concatenate([ks_buf[slot, p] for p in range(pg)], axis=1)
    s = _dot_nt(qlat, lat) + _dot(qr_ref[0], kr)
    s = s * ks_t * ATTN_SCALE
    m_prev = m_sc[...]
    m_new = jnp.maximum(m_prev, jnp.max(s, axis=-1, keepdims=True))
    alpha = jnp.exp(m_prev - m_new)
    p = jnp.exp(s - m_new)
    l_sc[...] = alpha * l_sc[...] + jnp.sum(p, axis=-1, keepdims=True)
    acc_sc[...] = alpha * acc_sc[...] + _dot(p.astype(BF16), lat)
    m_sc[...] = m_new

    @pl.when(c == nch - 1)
    def _():
        ckv = ckv_ref[0]
        s_new = (jnp.sum(qlat.astype(F32) * ckv, axis=-1, keepdims=True)
                 + jnp.sum(qr_ref[0].astype(F32) * krn_ref[0], axis=-1, keepdims=True))
        s_new = s_new * ksn_ref[0] * ATTN_SCALE
        m_prev = m_sc[...]
        m_new = jnp.maximum(m_prev, s_new)
        alpha = jnp.exp(m_prev - m_new)
        p_new = jnp.exp(s_new - m_new)
        l = alpha * l_sc[...] + p_new
        o_ref[0] = (alpha * acc_sc[...] + p_new * ckv) / l


def _decode_attention(page_table, qlat, qr, ckv, krn, ksn, lat_pool, kr_pool, ks_pool, pg):
    B, n_pages = page_table.shape
    nch = n_pages // pg
    tk = pg * PAGE
    kern = functools.partial(_decode_kernel, pg=pg, nch=nch)
    per_b = lambda shape: pl.BlockSpec((1,) + shape, lambda b, c, pt: (b, 0, 0))
    grid_spec = pltpu.PrefetchScalarGridSpec(
        num_scalar_prefetch=1, grid=(B, nch),
        in_specs=[per_b((N_HEADS, KV_RANK)), per_b((N_HEADS, QK_ROPE)), per_b((1, KV_RANK)), per_b((1, QK_ROPE)),
                  per_b((N_HEADS, 1)),
                  pl.BlockSpec(memory_space=pl.ANY), pl.BlockSpec(memory_space=pl.ANY),
                  pl.BlockSpec(memory_space=pl.ANY)],
        out_specs=per_b((N_HEADS, KV_RANK)),
        scratch_shapes=[pltpu.VMEM((2, tk, KV_RANK), F32), pltpu.VMEM((2, pg, QK_ROPE, PAGE), F32),
                        pltpu.VMEM((2, pg, N_HEADS, PAGE), F32),
                        pltpu.SemaphoreType.DMA((2, 3, pl.cdiv(pg, PAGES_PER_SEM))),
                        pltpu.VMEM((N_HEADS, 1), F32), pltpu.VMEM((N_HEADS, 1), F32),
                        pltpu.VMEM((N_HEADS, KV_RANK), F32)])
    return pl.pallas_call(
        kern, grid_spec=grid_spec, out_shape=jax.ShapeDtypeStruct((B, N_HEADS, KV_RANK), F32),
        compiler_params=_cparams(("arbitrary", "arbitrary")), name="decode_attention",
    )(page_table.reshape(-1), qlat, qr, ckv, krn, ksn, lat_pool, kr_pool, ks_pool)


def _sample_attention(q, ckv, krope, kscale, page_table, lat_pool, kr_pool, ks_pool, W, pg):
    B = q.shape[0]
    qlat = pl.pallas_call(_qlat_kernel, out_shape=jax.ShapeDtypeStruct((B, N_HEADS * KV_RANK), BF16),
                          name="q_absorb")(q, W['g_kn_nope'], W['w_uk_t'])
    qr = q.reshape(B, N_HEADS, LANES)[:, :, QK_NOPE:QK_DIM].astype(BF16)
    ol = _decode_attention(page_table, qlat.reshape(B, N_HEADS, KV_RANK), qr, ckv.reshape(B, 1, KV_RANK),
                           krope.reshape(B, 1, QK_ROPE), kscale.reshape(B, N_HEADS, 1),
                           lat_pool, jnp.swapaxes(kr_pool, 1, 2), jnp.swapaxes(ks_pool, 1, 2), pg)
    return pl.pallas_call(_uv_kernel, out_shape=jax.ShapeDtypeStruct((B, HP), F32),
                          name="value_up")(ol.reshape(B, N_HEADS * KV_RANK), W['w_uv_p'])


def _mix_kernel(attn_ref, ssm_ref, x_ref, ga_ref, gs_ref, woa_ref, wos_ref, gffn_ref, wq_ref,
                x1_ref, xn_ref, qp_ref):
    a = _rms(attn_ref[...], ga_ref[...], D_ATTN).astype(BF16)
    s = _rms(ssm_ref[...], gs_ref[...], D_SSM).astype(BF16)
    x1 = x_ref[...] + _dot(a, woa_ref[...]) + _dot(s, wos_ref[...])
    x1_ref[...] = x1
    xn = _rms(x1, gffn_ref[...], D_MODEL).astype(BF16)
    xn_ref[...] = xn
    qp_ref[...] = _dot(xn, wq_ref[...]).astype(BF16)


def _mix(attn, ssm, x, W, tm):
    T = x.shape[0]
    tok = lambda w: pl.BlockSpec((tm, w), lambda i: (i, 0))
    weights = [W['g_attn_p'], W['g_ssm_out'], W['w_out_a'], W['w_out_s'], W['g_ffn'], W['w_peer_q']]
    nq = 2 * PEER_HEADS * PEER_HALF
    return pl.pallas_call(
        _mix_kernel, grid=(T // tm,),
        in_specs=[tok(HP), tok(D_SSM), tok(D_MODEL)] + [_full(w.shape) for w in weights],
        out_specs=[tok(D_MODEL), tok(D_MODEL), tok(nq)],
        out_shape=[jax.ShapeDtypeStruct((T, D_MODEL), F32), jax.ShapeDtypeStruct((T, D_MODEL), BF16),
                   jax.ShapeDtypeStruct((T, nq), BF16)],
        compiler_params=_cparams(("parallel",)), name="mix",
    )(attn, ssm, x, *weights)


_CAND = [(a, b) for a in range(PEER_TOPK) for b in range(PEER_TOPK) if (a + 1) * (b + 1) <= PEER_TOPK]


def _top16_rows(s):
    vals = []
    for _ in range(PEER_TOPK):
        m = jnp.max(s, axis=0, keepdims=True)
        vals.append(m)
        s = jnp.where(s == m, NEG, s)
    return vals


def _top16_ranked(s):
    vals = []
    rank = jnp.full(s.shape, float(PEER_TOPK), F32)
    for k in range(PEER_TOPK):
        m = jnp.max(s, axis=0, keepdims=True)
        vals.append(m)
        hit = s == m
        rank = jnp.where(hit, float(k), rank)
        s = jnp.where(hit, NEG, s)
    return vals, rank


def _route_kernel(qp_ref, keys_ref, n_ref, e1_ref, r2_ref, e2_ref):
    for h in range(PEER_HEADS):
        q1 = qp_ref[:, (2 * h) * PEER_HALF:(2 * h + 1) * PEER_HALF]
        q2 = qp_ref[:, (2 * h + 1) * PEER_HALF:(2 * h + 2) * PEER_HALF]
        s1 = _dot_nt(keys_ref[2 * h], q1)
        s2 = _dot_nt(keys_ref[2 * h + 1], q2)
        v1 = _top16_rows(s1)
        v2, r2 = _top16_ranked(s2)
        cand = [v1[a] + v2[b] for a, b in _CAND]
        call = jnp.concatenate(cand, axis=0)
        tau = _top16_rows(call)[PEER_TOPK - 1]
        cmax = cand[0]
        z = jnp.sum(jnp.where(call >= tau, jnp.exp(call - cmax), 0.0), axis=0, keepdims=True)
        n = jnp.zeros(s1.shape, F32)
        for k2 in range(PEER_TOPK):
            n = n + jnp.where(s1 + v2[k2] >= tau, 1.0, 0.0)
        n_ref[h] = n
        e1_ref[h] = jnp.exp(s1 - v1[0]) * (0.5 / z)
        r2_ref[h] = r2.astype(BF16)
        e2_ref[h] = jnp.exp(s2 - v2[0]).astype(BF16)


def _route(qp, keys, tt):
    T = qp.shape[0]
    big = lambda dt: jax.ShapeDtypeStruct((PEER_HEADS, N_KEYS, T), dt)
    bspec = pl.BlockSpec((PEER_HEADS, N_KEYS, tt), lambda i: (0, 0, i))
    return pl.pallas_call(
        _route_kernel, grid=(T // tt,),
        in_specs=[pl.BlockSpec((tt, qp.shape[1]), lambda i: (i, 0)), _full(keys.shape)],
        out_specs=[bspec, bspec, bspec, bspec],
        out_shape=[big(F32), big(F32), big(BF16), big(BF16)],
        compiler_params=_cparams(("parallel",)), name="peer_route",
    )(qp, keys)


ROWS_PER_STEP = 8
E_TILE = ROWS_PER_STEP * N_KEYS


def _peer_kernel(xn_ref, u_ref, vt_ref, n_ref, e1_ref, r2_ref, e2_ref, y_ref, acc_sc, at_sc):
    j = pl.program_id(1)

    @pl.when(j == 0)
    def _():
        acc_sc[...] = jnp.zeros(acc_sc.shape, F32)

    st = _dot_nt(u_ref[...], xn_ref[...])
    tt = st.shape[1]
    zero = jnp.zeros((), BF16)
    groups = N_KEYS // BF16_ROWS

    def row16(ref, h, r):
        return jnp.broadcast_to(ref[h, r:r + 1, :], (BF16_ROWS, tt)).astype(BF16)[None]

    for r in range(ROWS_PER_STEP):
        w = None
        for h in range(PEER_HEADS):
            r2 = r2_ref[h].reshape(groups, BF16_ROWS, tt)
            e2 = e2_ref[h].reshape(groups, BF16_ROWS, tt)
            g = jnp.where(r2 < row16(n_ref, h, r), e2 * row16(e1_ref, h, r), zero)
            w = g if w is None else w + g
        rows = slice(r * N_KEYS, (r + 1) * N_KEYS)
        at_sc[rows, :] = _gelu_x2(st[rows, :]).astype(BF16) * w.reshape(N_KEYS, tt)
    acc_sc[...] += _dot(vt_ref[...], at_sc[...])

    @pl.when(j == pl.num_programs(1) - 1)
    def _():
        y_ref[...] = acc_sc[...].T


def _peer_dense(xn, u_tab, vt_tab, n, e1, r2, e2, tt):
    T = xn.shape[0]
    rows = pl.BlockSpec((PEER_HEADS, ROWS_PER_STEP, tt), lambda i, j: (0, j, i))
    keys = pl.BlockSpec((PEER_HEADS, N_KEYS, tt), lambda i, j: (0, 0, i))
    return pl.pallas_call(
        _peer_kernel, grid=(T // tt, N_EXPERTS // E_TILE),
        in_specs=[pl.BlockSpec((tt, D_MODEL), lambda i, j: (i, 0)),
                  pl.BlockSpec((E_TILE, D_MODEL), lambda i, j: (j, 0)),
                  pl.BlockSpec((D_MODEL, E_TILE), lambda i, j: (0, j)),
                  rows, rows, keys, keys],
        out_specs=pl.BlockSpec((tt, D_MODEL), lambda i, j: (i, 0)),
        out_shape=jax.ShapeDtypeStruct((T, D_MODEL), F32),
        scratch_shapes=[pltpu.VMEM((D_MODEL, tt), F32), pltpu.VMEM((E_TILE, tt), BF16)],
        compiler_params=_cparams(("parallel", "arbitrary")), name="peer_experts",
    )(xn, u_tab, vt_tab, n, e1, r2, e2)


def _final_kernel(x1_ref, y_ref, p_ref, wple_ref, gple_ref, wgate_ref, o_ref):
    x2 = x1_ref[...] + y_ref[...]
    e = _dot(p_ref[...].astype(BF16), wple_ref[...])
    gate = _sigmoid(_dot(_rms(x2, gple_ref[...], D_MODEL).astype(BF16), wgate_ref[...]))
    o_ref[...] = x2 + e * gate


def _final(x1, y, p, W, tm):
    T = x1.shape[0]
    tok = lambda w: pl.BlockSpec((tm, w), lambda i: (i, 0))
    weights = [W['w_ple'], W['g_ple'], W['w_ple_gate']]
    return pl.pallas_call(
        _final_kernel, grid=(T // tm,),
        in_specs=[tok(D_MODEL), tok(D_MODEL), tok(PLE_DIM)] + [_full(w.shape) for w in weights],
        out_specs=tok(D_MODEL), out_shape=jax.ShapeDtypeStruct((T, D_MODEL), F32),
        compiler_params=_cparams(("parallel",)), name="ple_gate",
    )(x1, y, p, *weights)


def _pad_heads(w, width):
    pad = [(0, 0)] * (w.ndim - 1) + [(0, LANES - width)]
    w = jnp.pad(w, pad)
    return w.reshape(w.shape[:-2] + (HP,))


def _rope_tables(pos):
    inv = ROPE_THETA ** (-jnp.arange(0, QK_ROPE, 2, dtype=F32) / QK_ROPE)
    ang = pos.astype(F32)[:, None] * inv[None, :]
    cos, sin = jnp.cos(ang), jnp.sin(ang)
    T = pos.shape[0]
    cos_t = jnp.concatenate([jnp.ones((T, QK_NOPE), F32), cos, cos, jnp.zeros((T, LANES - QK_DIM), F32)], axis=1)
    sin_t = jnp.concatenate([jnp.zeros((T, QK_NOPE), F32), -sin, sin, jnp.zeros((T, LANES - QK_DIM), F32)], axis=1)
    return cos_t, sin_t


def _swap_rope(w):
    z = jnp.zeros_like(w[..., :QK_NOPE])
    return jnp.concatenate([z, w[..., 80:96], w[..., 64:80]], axis=-1)


def _prep_common(P):
    W = {}
    row = lambda g: g.reshape(1, -1).astype(F32)
    w_in = P['w_in']
    kr = w_in[:, 640:672]
    z64 = jnp.zeros((D_MODEL, QK_NOPE), F32)
    z32 = jnp.zeros((D_MODEL, LANES - QK_DIM), F32)
    W['w_in_p'] = jnp.concatenate([w_in[:, :640], w_in[:, 672:], z64, kr, z32,
                                   z64, kr[:, 16:], kr[:, :16], z32], axis=1).astype(BF16)
    W['g_mix'] = row(P['g_mix'])
    W['g_q_lat'] = row(P['g_q_lat'])
    W['w_uq_p'] = _pad_heads(P['w_uq'], QK_DIM).astype(BF16)
    W['w_uq_sw'] = _pad_heads(_swap_rope(P['w_uq']), QK_DIM).astype(BF16)
    g_qn = P['g_qn']
    W['g_qn_p'] = jnp.pad(g_qn, (0, LANES - QK_DIM)).reshape(1, LANES)
    W['g_qn_sw'] = jnp.pad(_swap_rope(g_qn), (0, LANES - QK_DIM)).reshape(1, LANES)
    lane_head = np.arange(HP) // LANES
    W['ind'] = jnp.asarray(lane_head[:, None] == np.arange(LANES)[None, :], BF16)
    W['expand'] = jnp.asarray(np.arange(LANES)[:, None] == lane_head[None, :], BF16)
    W['g_kv_lat'] = row(P['g_kv_lat'])
    W['w_uk_p'] = _pad_heads(P['w_uk'], QK_NOPE).astype(BF16)
    W['w_uv_p'] = _pad_heads(P['w_uv'], V_DIM).astype(BF16)
    W['w_uk_t'] = jnp.pad(jnp.transpose(P['w_uk'], (1, 2, 0)),
                          ((0, 0), (0, LANES - QK_NOPE), (0, 0))).astype(BF16)
    g_kn = P['g_kn']
    W['g_kn_nope'] = jnp.pad(g_kn[:QK_NOPE], (0, LANES - QK_NOPE)).reshape(1, LANES)
    W['g_kc'] = jnp.pad(g_kn, (0, LANES - QK_DIM)).at[:QK_NOPE].set(0.0).reshape(1, LANES)
    W['g_ks'] = jnp.pad(_swap_rope(g_kn), (0, LANES - QK_DIM)).reshape(1, LANES)

    W['g_attn_p'] = _pad_heads(P['g_attn_out'].reshape(N_HEADS, V_DIM), V_DIM).reshape(1, HP)
    W['g_ssm_out'] = row(P['g_ssm_out'])
    w_out = P['w_out']
    W['w_out_a'] = jnp.pad(w_out[:D_ATTN].reshape(N_HEADS, V_DIM, D_MODEL),
                           ((0, 0), (0, LANES - V_DIM), (0, 0))).reshape(HP, D_MODEL).astype(BF16)
    W['w_out_s'] = w_out[D_ATTN:].astype(BF16)
    W['g_ffn'] = row(P['g_ffn'])
    W['w_peer_q'] = P['w_peer_q'].astype(BF16)
    W['peer_keys'] = P['peer_keys'].reshape(2 * PEER_HEADS, N_KEYS, PEER_HALF).astype(BF16)
    W['peer_u'] = P['peer_u'].astype(BF16)
    W['peer_vt'] = P['peer_v'].T.astype(BF16)
    W['w_ple'] = P['w_ple'].astype(BF16)
    W['g_ple'] = row(P['g_ple'])
    W['w_ple_gate'] = P['w_ple_gate'].astype(BF16)
    return W


def _ffn_tail(attn, ssm, x, p, W, tm, tt_route, tt_peer):
    x1, xn, qp = _mix(attn, ssm, x, W, tm)
    n, e1, r2, e2 = _route(qp, W['peer_keys'], tt_route)
    y = _peer_dense(xn, W['peer_u'], W['peer_vt'], n, e1, r2, e2, tt_peer)
    return _final(x1, y, p, W, tm)


_PARAM_NAMES = ('g_mix', 'w_in', 'g_q_lat', 'w_uq', 'g_kv_lat', 'w_uk', 'w_uv', 'g_qn', 'g_kn', 'ssm_a_re',
                'ssm_a_im', 'ssm_log_dt', 'ssm_b_re', 'ssm_b_im', 'ssm_c_re', 'ssm_c_im', 'ssm_d', 'w_glu', 'b_glu',
                'g_attn_out', 'g_ssm_out', 'w_out', 'g_ffn', 'w_peer_q', 'peer_keys', 'peer_u', 'peer_v', 'g_ple',
                'w_ple', 'w_ple_gate')

TM_TOKEN = 512
TQ_FLASH = 1024
TK_FLASH = 1024
TS_FLASH = 512
TB_SCAN = 512
TT_ROUTE = 256
TT_PEER = 1024
PAGES_PER_STEP = 32
PAGES_PER_SEM = 16
DMA_ISSUE_UNROLL = 4


def _layer(x_p, x_s, p_p, p_s, caches, states, page_table, P):
    W = _prep_common(P)
    W.update(_prep_ssm(P))
    T = x_p.shape[0]
    B = x_s.shape[0]
    n_past = page_table.shape[1] * PAGE

    cos_t, sin_t = _rope_tables(jnp.arange(T))
    q, k, v, ckv_p, kr_p, ks_p, u = _project(x_p, cos_t, sin_t, W, BF16, TM_TOKEN)
    attn = _flash_attention(q, k, v, TQ_FLASH, TK_FLASH, TS_FLASH)
    ssm, hre_p, him_p = _s5_prompt(u, W, TB_SCAN)
    y_p = _ffn_tail(attn, ssm, x_p, p_p, W, TM_TOKEN, TT_ROUTE, TT_PEER)

    cos_s, sin_s = _rope_tables(jnp.full((B,), n_past, jnp.int32))
    q, _, _, ckv_s, kr_s, ks_s, u = _project(x_s, cos_s, sin_s, W, F32, B)
    attn = _sample_attention(q, ckv_s, kr_s, ks_s, page_table, *caches, W, PAGES_PER_STEP)
    ssm, hre_s, him_s = _s5_step(u, states[0].reshape(B, N_STATE), states[1].reshape(B, N_STATE), W)
    y_s = _ffn_tail(attn, ssm, x_s, p_s, W, B, B, B)
    st = lambda a, lead: a.reshape(lead + (N_GROUPS, SSM_STATE))
    return (y_p, y_s, (ckv_p, kr_p, ks_p, st(hre_p, (1,)), st(him_p, (1,))),
            (ckv_s, kr_s, ks_s, st(hre_s, (B,)), st(him_s, (B,))))


def kernel(x_prompt, x_sample, cache_kv_latent, cache_k_rope, cache_k_scale, state_ssm_re, state_ssm_im,
           page_table, p_prompt, p_sample, g_mix, w_in, g_q_lat, w_uq, g_kv_lat, w_uk, w_uv, g_qn, g_kn,
           ssm_a_re, ssm_a_im, ssm_log_dt, ssm_b_re, ssm_b_im, ssm_c_re, ssm_c_im, ssm_d, w_glu, b_glu,
           g_attn_out, g_ssm_out, w_out, g_ffn, w_peer_q, peer_keys, peer_u, peer_v, g_ple, w_ple, w_ple_gate):
    params = dict(zip(_PARAM_NAMES, (g_mix, w_in, g_q_lat, w_uq, g_kv_lat, w_uk, w_uv, g_qn, g_kn, ssm_a_re,
                                     ssm_a_im, ssm_log_dt, ssm_b_re, ssm_b_im, ssm_c_re, ssm_c_im, ssm_d, w_glu,
                                     b_glu, g_attn_out, g_ssm_out, w_out, g_ffn, w_peer_q, peer_keys, peer_u,
                                     peer_v, g_ple, w_ple, w_ple_gate)))
    depth = g_mix.shape[0]
    assert x_prompt.shape[0] == 1 and x_sample.shape[1] == 1
    x_p = x_prompt[0]
    x_s = x_sample[:, 0]
    outs_p, outs_s = [], []
    for i in range(depth):
        P = {name: val[i] for name, val in params.items()}
        x_p, x_s, o_p, o_s = _layer(x_p, x_s, p_prompt[i, 0], p_sample[i, :, 0],
                                    (cache_kv_latent[i], cache_k_rope[i], cache_k_scale[i]),
                                    (state_ssm_re[i], state_ssm_im[i]), page_table, P)
        outs_p.append(o_p)
        outs_s.append(o_s)
    stack_p = [jnp.stack([o[j] for o in outs_p])[:, None] for j in range(5)]
    stack_s = [jnp.stack([o[j] for o in outs_s]) for j in range(5)]
    stack_s = [a[:, :, None] if j < 3 else a for j, a in enumerate(stack_s)]
    return (x_p[None], x_s[:, None], stack_p[0], stack_p[1], stack_p[2], stack_p[3][:, 0], stack_p[4][:, 0],
            stack_s[0], stack_s[1], stack_s[2], stack_s[3], stack_s[4])
```

```python
import functools
import math

import jax
import jax.numpy as jnp
import numpy as np
from jax import lax
from jax.experimental import pallas as pl
from jax.experimental.pallas import tpu as pltpu

F32 = jnp.float32
BF16 = jnp.bfloat16

D_MODEL = 1024
N_HEADS = 8
QK_NOPE = 64
QK_ROPE = 32
QK_DIM = QK_NOPE + QK_ROPE
V_DIM = 64
D_ATTN = N_HEADS * V_DIM
Q_RANK = 384
KV_RANK = 256
ROPE_THETA = 10000.0
ATTN_SCALE = QK_DIM ** -0.5
D_SSM = 512
SSM_GROUP = 16
N_GROUPS = 32
SSM_STATE = 64
N_STATE = N_GROUPS * SSM_STATE
PEER_HEADS = 8
N_KEYS = 128
N_EXPERTS = N_KEYS * N_KEYS
PEER_TOPK = 16
PEER_HALF = 128
PLE_DIM = 256
PAGE = 128
EPS = 1e-6

LANES = 128
BF16_ROWS = 16
HP = N_HEADS * LANES
NEG = -1e30
VMEM_LIMIT = 56 * 1024 * 1024


def _cparams(sem, vmem=VMEM_LIMIT):
    return pltpu.CompilerParams(dimension_semantics=sem, vmem_limit_bytes=vmem)


def _dot(a, b):
    return jnp.dot(a, b, preferred_element_type=F32)


def _dot_nt(a, b):
    return lax.dot_general(a, b, (((1,), (1,)), ((), ())), preferred_element_type=F32)


def _rms(x, g, n):
    ms = jnp.sum(x * x, axis=-1, keepdims=True) * (1.0 / n)
    return x * lax.rsqrt(ms + EPS) * g


def _gelu(x):
    return 0.5 * x * (1.0 + jnp.tanh(0.7978845608028654 * (x + 0.044715 * (x * x * x))))


def _gelu_x2(x):
    c0 = 0.7978845608028654
    t = jnp.tanh(x * (c0 + (c0 * 0.044715) * (x * x)))
    return x + x * t


def _sigmoid(x):
    return 1.0 / (1.0 + jnp.exp(-x))


def _full(shape):
    return pl.BlockSpec(shape, lambda *_: (0,) * len(shape))


def _proj_kernel(x_ref, cos_ref, sin_ref, gmix_ref, win_ref, gq_ref, wuq_ref, wuqs_ref, gqn_ref, gqns_ref,
                 gkv_ref, wuk_ref, wuv_ref, gknn_ref, gkc_ref, gks_ref,
                 q_ref, k_ref, v_ref, ckv_ref, krope_ref, kscale_ref, u_ref):
    x = x_ref[...]
    cos = cos_ref[...]
    sin = sin_ref[...]
    h = _rms(x, gmix_ref[...], D_MODEL).astype(BF16)
    proj = _dot(h, win_ref[...])
    cq = proj[:, :Q_RANK]
    ckv = proj[:, Q_RANK:Q_RANK + KV_RANK]
    u_ref[...] = proj[:, 640:1152]
    krt = proj[:, 1152:1280]
    krts = proj[:, 1280:1408]

    cqn = _rms(cq, gq_ref[...], Q_RANK).astype(BF16)
    q_raw = _dot(cqn, wuq_ref[...])
    q_sw = _dot(cqn, wuqs_ref[...])
    gqn = gqn_ref[...]
    gqns = gqns_ref[...]

    ckvn = _rms(ckv, gkv_ref[...], KV_RANK)
    ckv_ref[...] = ckvn
    ckvb = ckvn.astype(BF16)
    kn_raw = _dot(ckvb, wuk_ref[...])
    v_ref[...] = _dot(ckvb, wuv_ref[...]).astype(v_ref.dtype)
    ms_rope = jnp.sum(krt * krt, axis=-1, keepdims=True)
    kr_tile = (krt * gkc_ref[...]) * cos + (krts * gks_ref[...]) * sin
    krope_ref[...] = kr_tile[:, QK_NOPE:QK_DIM]
    gknn = gknn_ref[...]
    lane = lax.broadcasted_iota(jnp.int32, (1, LANES), 1)
    ks_all = jnp.zeros((x.shape[0], LANES), F32)
    for hd in range(N_HEADS):
        sl = slice(hd * LANES, (hd + 1) * LANES)
        qh = q_raw[:, sl]
        rq = lax.rsqrt(jnp.sum(qh * qh, axis=-1, keepdims=True) * (1.0 / QK_DIM) + EPS)
        q_ref[:, sl] = ((qh * rq * gqn) * cos + (q_sw[:, sl] * rq * gqns) * sin).astype(q_ref.dtype)
        kh = kn_raw[:, sl]
        ks = lax.rsqrt((jnp.sum(kh * kh, axis=-1, keepdims=True) + ms_rope) * (1.0 / QK_DIM) + EPS)
        k_ref[:, sl] = ((kh * gknn + kr_tile) * ks).astype(k_ref.dtype)
        ks_all = jnp.where(lane == hd, ks, ks_all)
    kscale_ref[...] = ks_all[:, :N_HEADS]


def _project(x, cos_t, sin_t, W, q_dtype, tm):
    T = x.shape[0]
    tok = lambda w: pl.BlockSpec((tm, w), lambda i: (i, 0))
    weights = [W['g_mix'], W['w_in_p'], W['g_q_lat'], W['w_uq_p'], W['w_uq_sw'], W['g_qn_p'], W['g_qn_sw'],
               W['g_kv_lat'], W['w_uk_p'], W['w_uv_p'], W['g_kn_nope'], W['g_kc'], W['g_ks']]
    out_shape = [jax.ShapeDtypeStruct((T, HP), q_dtype), jax.ShapeDtypeStruct((T, HP), BF16),
                 jax.ShapeDtypeStruct((T, HP), BF16), jax.ShapeDtypeStruct((T, KV_RANK), F32),
                 jax.ShapeDtypeStruct((T, QK_ROPE), F32), jax.ShapeDtypeStruct((T, N_HEADS), F32),
                 jax.ShapeDtypeStruct((T, D_SSM), F32)]
    return pl.pallas_call(
        _proj_kernel, grid=(T // tm,),
        in_specs=[tok(D_MODEL), tok(LANES), tok(LANES)] + [_full(w.shape) for w in weights],
        out_specs=[tok(HP), tok(HP), tok(HP), tok(KV_RANK), tok(QK_ROPE), tok(N_HEADS), tok(D_SSM)],
        out_shape=out_shape, compiler_params=_cparams(("parallel",)), name="project",
    )(x, cos_t, sin_t, *weights)


SCORE_SCALE = ATTN_SCALE * math.log2(math.e)


def _flash_kernel(q_ref, k_ref, v_ref, o_ref, m_sc, l_sc, acc_sc, *, tq, tk, ts):
    i = pl.program_id(1)
    nsub = tq // ts
    m_sc[...] = jnp.full(m_sc.shape, NEG, F32)
    l_sc[...] = jnp.zeros(l_sc.shape, F32)
    acc_sc[...] = jnp.zeros(acc_sc.shape, F32)
    lane_tiles = tk // LANES

    def sub_step(a, j, k, v, masked):
        rows = slice(a * ts, (a + 1) * ts)
        s = _dot_nt(q_ref[rows, :], k)
        if masked:
            r = i * tq + a * ts + lax.broadcasted_iota(jnp.int32, (ts, tk), 0)
            c = j * tk + lax.broadcasted_iota(jnp.int32, (ts, tk), 1)
            s = jnp.where(c <= r, s, NEG)
        m_prev = m_sc[rows, :]
        m_new = jnp.maximum(m_prev, jnp.max(s, axis=-1, keepdims=True))
        alpha = jnp.exp2((m_prev - m_new) * SCORE_SCALE)
        p = jnp.exp2((s - jnp.concatenate([m_new] * lane_tiles, axis=1)) * SCORE_SCALE)
        l_sc[rows, :] = alpha * l_sc[rows, :] + jnp.sum(p, axis=-1, keepdims=True)
        acc_sc[rows, :] = alpha * acc_sc[rows, :] + _dot(p.astype(BF16), v)
        m_sc[rows, :] = m_new

    def full_step(j, carry):
        start = pl.multiple_of(j * tk, tk)
        k = k_ref[pl.ds(start, tk), :]
        v = v_ref[pl.ds(start, tk), :]
        for a in range(nsub):
            sub_step(a, j, k, v, False)
        return carry

    n_full = i * (tq // tk)
    lax.fori_loop(0, n_full, full_step, 0)
    for jj in range(tq // tk):
        j = n_full + jj
        start = pl.multiple_of(j * tk, tk)
        k = k_ref[pl.ds(start, tk), :]
        v = v_ref[pl.ds(start, tk), :]
        for a in range(nsub):
            lo_row, hi_row = a * ts, (a + 1) * ts - 1
            lo_col, hi_col = jj * tk, (jj + 1) * tk - 1
            if lo_col > hi_row:
                continue
            sub_step(a, j, k, v, hi_col > lo_row)
    o_ref[...] = acc_sc[...] / l_sc[...]


def _flash_attention(q, k, v, tq, tk, ts):
    T = q.shape[0]
    kern = functools.partial(_flash_kernel, tq=tq, tk=tk, ts=ts)
    return pl.pallas_call(
        kern, grid=(N_HEADS, T // tq),
        in_specs=[pl.BlockSpec((tq, LANES), lambda h, i: (i, h)),
                  pl.BlockSpec((T, LANES), lambda h, i: (0, h)),
                  pl.BlockSpec((T, LANES), lambda h, i: (0, h))],
        out_specs=pl.BlockSpec((tq, LANES), lambda h, i: (i, h)),
        out_shape=jax.ShapeDtypeStruct((T, HP), F32),
        scratch_shapes=[pltpu.VMEM((tq, LANES), F32), pltpu.VMEM((tq, LANES), F32), pltpu.VMEM((tq, LANES), F32)],
        compiler_params=_cparams(("parallel", "arbitrary")), name="flash_attention",
    )(q, k, v)


def _cmul(ar, ai, br, bi):
    return ar * br - ai * bi, ar * bi + ai * br


def _ssm_disc_kernel(are_ref, aim_ref, dt_ref, bre_ref, bim_ref, abr_ref, abi_ref, bbr_ref, bbi_ref):
    lr = are_ref[...]
    li = aim_ref[...]
    dt = dt_ref[...]
    mag = jnp.exp(lr * dt)
    abr = mag * jnp.cos(li * dt)
    abi = mag * jnp.sin(li * dt)
    abr_ref[...] = abr
    abi_ref[...] = abi
    inv = 1.0 / (lr * lr + li * li)
    cr, ci = _cmul(abr - 1.0, abi, lr * inv, -li * inv)
    bbr, bbi = _cmul(cr, ci, bre_ref[...], bim_ref[...])
    bbr_ref[...] = bbr
    bbi_ref[...] = bbi


def _ssm_discretise(P):
    col = lambda a: a.reshape(N_STATE, 1)
    dt = jnp.repeat(jnp.exp(P['ssm_log_dt']), SSM_STATE).reshape(N_STATE, 1)
    bre = P['ssm_b_re'].reshape(N_STATE, SSM_GROUP)
    bim = P['ssm_b_im'].reshape(N_STATE, SSM_GROUP)
    s1 = jax.ShapeDtypeStruct((N_STATE, 1), F32)
    s16 = jax.ShapeDtypeStruct((N_STATE, SSM_GROUP), F32)
    return pl.pallas_call(_ssm_disc_kernel, out_shape=[s1, s1, s16, s16], name="ssm_discretise")(
        col(P['ssm_a_re']), col(P['ssm_a_im']), dt, bre, bim)


def _prep_ssm(P):
    abr, abi, bbr, bbi = _ssm_discretise(P)
    eye = jnp.eye(N_GROUPS, dtype=F32)

    def bdiag_in(b):
        b = b.reshape(N_GROUPS, SSM_STATE, SSM_GROUP)
        return jnp.einsum('gpc,gh->gchp', b, eye).reshape(D_SSM, N_STATE)

    def bdiag_out(c):
        return jnp.einsum('gcp,gh->gphc', c, eye).reshape(N_STATE, D_SSM)

    W = {}
    W['ssm_bm'] = jnp.concatenate([bdiag_in(bbr), bdiag_in(bbi)], axis=1)
    W['ssm_cm'] = jnp.concatenate([bdiag_out(P['ssm_c_re']), -bdiag_out(P['ssm_c_im'])], axis=0)
    W['ssm_ar'] = abr.reshape(1, N_STATE)
    W['ssm_ai'] = abi.reshape(1, N_STATE)
    W['ssm_d'] = P['ssm_d'].reshape(1, D_SSM)
    W['w_glu'] = P['w_glu'].astype(BF16)
    W['b_glu'] = P['b_glu'].reshape(1, D_SSM)
    return W


SCAN_LANES = 512


def _s5_prompt_kernel(u_ref, perm_ref, permt_ref, bm_ref, cm_ref, ar_ref, ai_ref, d_ref, wglu_ref, bglu_ref,
                      out_ref, hre_ref, him_ref, x_sc, carry_sc, *, tb):
    n = tb // 8
    nk = N_STATE // SCAN_LANES
    step = pl.program_id(0)

    @pl.when(step == 0)
    def _():
        carry_sc[...] = jnp.zeros(carry_sc.shape, F32)

    u = u_ref[...]
    up = _dot(perm_ref[...], u.astype(BF16)).astype(BF16)
    for k in range(nk):
        uk = up[:, k * LANES:(k + 1) * LANES]
        for c in range(2):
            lo = c * N_STATE + k * SCAN_LANES
            x_sc[:, lo:lo + SCAN_LANES] = _dot(uk, bm_ref[k * LANES:(k + 1) * LANES, lo:lo + SCAN_LANES])

    for k in range(nk):
        re_sl = slice(k * SCAN_LANES, (k + 1) * SCAN_LANES)
        im_sl = slice(N_STATE + k * SCAN_LANES, N_STATE + (k + 1) * SCAN_LANES)
        ar = jnp.broadcast_to(ar_ref[:, re_sl], (8, SCAN_LANES))
        ai = jnp.broadcast_to(ai_ref[:, re_sl], (8, SCAN_LANES))

        def local_scan(i, h):
            rows = pl.ds(pl.multiple_of(i * 8, 8), 8)
            pr, pi = _cmul(ar, ai, h[0], h[1])
            hr = pr + x_sc[rows, re_sl]
            hi = pi + x_sc[rows, im_sl]
            x_sc[rows, re_sl] = hr
            x_sc[rows, im_sl] = hi
            return hr, hi

        zero = jnp.zeros((8, SCAN_LANES), F32)
        er, ei = lax.fori_loop(0, n, local_scan, (zero, zero))

        pr, pi = ar, ai
        for _ in range(int(math.log2(n))):
            pr, pi = _cmul(pr, pi, pr, pi)
        cr = [carry_sc[0:1, re_sl]]
        ci = [carry_sc[1:2, re_sl]]
        for s in range(1, 9):
            qr, qi = _cmul(pr[0:1], pi[0:1], cr[-1], ci[-1])
            cr.append(qr + er[s - 1:s])
            ci.append(qi + ei[s - 1:s])
        carry_sc[0:1, re_sl] = cr[8]
        carry_sc[1:2, re_sl] = ci[8]
        cin = (jnp.concatenate(cr[:8], axis=0), jnp.concatenate(ci[:8], axis=0))

        def patch(i, c):
            rows = pl.ds(pl.multiple_of(i * 8, 8), 8)
            c = _cmul(ar, ai, c[0], c[1])
            x_sc[rows, re_sl] = x_sc[rows, re_sl] + c[0]
            x_sc[rows, im_sl] = x_sc[rows, im_sl] + c[1]
            return c

        lax.fori_loop(0, n, patch, cin)

    ys = []
    for k in range(nk):
        zr = x_sc[:, k * SCAN_LANES:(k + 1) * SCAN_LANES].astype(BF16)
        zi = x_sc[:, N_STATE + k * SCAN_LANES:N_STATE + (k + 1) * SCAN_LANES].astype(BF16)
        ys.append(_dot(zr, cm_ref[k * SCAN_LANES:(k + 1) * SCAN_LANES, k * LANES:(k + 1) * LANES])
                  + _dot(zi, cm_ref[N_STATE + k * SCAN_LANES:N_STATE + (k + 1) * SCAN_LANES,
                                    k * LANES:(k + 1) * LANES]))
    yp = jnp.concatenate(ys, axis=1)
    pt = permt_ref[...]
    hi = yp.astype(BF16)
    r1 = yp - hi.astype(F32)
    mid = r1.astype(BF16)
    lo = (r1 - mid.astype(F32)).astype(BF16)
    y = _dot(pt, hi) + _dot(pt, mid) + _dot(pt, lo) + d_ref[...] * u
    g = _gelu(y)
    out_ref[...] = g * _sigmoid(_dot(g.astype(BF16), wglu_ref[...]) + bglu_ref[...])
    hre_ref[...] = carry_sc[0:1, :]
    him_ref[...] = carry_sc[1:2, :]


def _s5_prompt(u, W, tb):
    T = u.shape[0]
    n = tb // 8
    r = np.arange(tb)
    src = (r % 8) * n + r // 8
    perm = np.zeros((tb, tb), np.float32)
    perm[r, src] = 1.0
    kern = functools.partial(_s5_prompt_kernel, tb=tb)
    weights = [jnp.asarray(perm, BF16), jnp.asarray(perm.T, BF16), W['ssm_bm'].astype(BF16),
               W['ssm_cm'].astype(BF16), W['ssm_ar'], W['ssm_ai'], W['ssm_d'], W['w_glu'], W['b_glu']]
    return pl.pallas_call(
        kern, grid=(T // tb,),
        in_specs=[pl.BlockSpec((tb, D_SSM), lambda i: (i, 0))] + [_full(w.shape) for w in weights],
        out_specs=[pl.BlockSpec((tb, D_SSM), lambda i: (i, 0)), _full((1, N_STATE)), _full((1, N_STATE))],
        out_shape=[jax.ShapeDtypeStruct((T, D_SSM), F32), jax.ShapeDtypeStruct((1, N_STATE), F32),
                   jax.ShapeDtypeStruct((1, N_STATE), F32)],
        scratch_shapes=[pltpu.VMEM((tb, 2 * N_STATE), F32), pltpu.VMEM((2, N_STATE), F32)],
        compiler_params=_cparams(("arbitrary",)), name="s5_prompt",
    )(u, *weights)


def _s5_step_kernel(u_ref, hre_ref, him_ref, bm_ref, cm_ref, ar_ref, ai_ref, d_ref, wglu_ref, bglu_ref,
                    out_ref, nre_ref, nim_ref):
    u = u_ref[...]
    bm = bm_ref[...]
    uh = u.astype(BF16)
    ul = (u - uh.astype(F32)).astype(BF16)
    bh = bm.astype(BF16)
    bl = (bm - bh.astype(F32)).astype(BF16)
    bu = _dot(uh, bh) + _dot(uh, bl) + _dot(ul, bh)
    pr, pi = _cmul(ar_ref[...], ai_ref[...], hre_ref[...], him_ref[...])
    hr = pr + bu[:, :N_STATE]
    hi = pi + bu[:, N_STATE:]
    nre_ref[...] = hr
    nim_ref[...] = hi
    cm = cm_ref[...]
    y = _dot(hr.astype(BF16), cm[:N_STATE]) + _dot(hi.astype(BF16), cm[N_STATE:]) + d_ref[...] * u
    g = _gelu(y)
    out_ref[...] = g * _sigmoid(_dot(g.astype(BF16), wglu_ref[...]) + bglu_ref[...])


def _s5_step(u, hre, him, W):
    B = u.shape[0]
    st = jax.ShapeDtypeStruct((B, N_STATE), F32)
    return pl.pallas_call(
        _s5_step_kernel, out_shape=[jax.ShapeDtypeStruct((B, D_SSM), F32), st, st],
        compiler_params=pltpu.CompilerParams(vmem_limit_bytes=VMEM_LIMIT), name="s5_step",
    )(u, hre, him, W['ssm_bm'], W['ssm_cm'].astype(BF16), W['ssm_ar'], W['ssm_ai'], W['ssm_d'],
      W['w_glu'], W['b_glu'])


def _qlat_kernel(q_ref, gknn_ref, wukt_ref, o_ref):
    g = gknn_ref[...]
    for h in range(N_HEADS):
        qh = (q_ref[:, h * LANES:(h + 1) * LANES] * g).astype(BF16)
        o_ref[:, h * KV_RANK:(h + 1) * KV_RANK] = _dot(qh, wukt_ref[h]).astype(o_ref.dtype)


def _uv_kernel(ol_ref, wuv_ref, o_ref):
    for h in range(N_HEADS):
        o_ref[:, h * LANES:(h + 1) * LANES] = _dot(ol_ref[:, h * KV_RANK:(h + 1) * KV_RANK].astype(BF16),
                                                   wuv_ref[:, h * LANES:(h + 1) * LANES])


def _decode_kernel(pt_ref, qlat_ref, qr_ref, ckv_ref, krn_ref, ksn_ref, lat_hbm, kr_hbm, ks_hbm, o_ref,
                   lat_buf, kr_buf, ks_buf, sem, m_sc, l_sc, acc_sc, *, pg, nch):
    b = pl.program_id(0)
    c = pl.program_id(1)
    g = b * nch + c
    total = pl.num_programs(0) * nch

    def copies(step, slot, p):
        page = pt_ref[step * pg + p]
        grp = p // PAGES_PER_SEM
        return (pltpu.make_async_copy(lat_hbm.at[page], lat_buf.at[slot, pl.ds(p * PAGE, PAGE)],
                                      sem.at[slot, 0, grp]),
                pltpu.make_async_copy(kr_hbm.at[page], kr_buf.at[slot, p], sem.at[slot, 1, grp]),
                pltpu.make_async_copy(ks_hbm.at[page], ks_buf.at[slot, p], sem.at[slot, 2, grp]))

    n_groups = pl.cdiv(pg, PAGES_PER_SEM)

    def group_pages(grp):
        first = grp * PAGES_PER_SEM
        return first, min(PAGES_PER_SEM, pg - first)

    def start_group(step, slot, grp):
        first, count = group_pages(grp)

        def body(p, carry):
            for d in copies(step, slot, p):
                d.start()
            return carry
        lax.fori_loop(first, first + count, body, 0, unroll=DMA_ISSUE_UNROLL)

    def wait_group(slot, grp):
        first, count = group_pages(grp)
        lat_grp = lat_buf.at[slot, pl.ds(first * PAGE, count * PAGE)]
        kr_grp = kr_buf.at[slot, pl.ds(first, count)]
        ks_grp = ks_buf.at[slot, pl.ds(first, count)]
        pltpu.make_async_copy(lat_grp, lat_grp, sem.at[slot, 0, grp]).wait()
        pltpu.make_async_copy(kr_grp, kr_grp, sem.at[slot, 1, grp]).wait()
        pltpu.make_async_copy(ks_grp, ks_grp, sem.at[slot, 2, grp]).wait()

    @pl.when(g == 0)
    def _():
        for grp in range(n_groups):
            start_group(0, 0, grp)

    @pl.when(c == 0)
    def _():
        m_sc[...] = jnp.full(m_sc.shape, NEG, F32)
        l_sc[...] = jnp.zeros(l_sc.shape, F32)
        acc_sc[...] = jnp.zeros(acc_sc.shape, F32)

    slot = g % 2
    for grp in range(n_groups):
        wait_group(slot, grp)

        @pl.when(g + 1 < total)
        def _():
            start_group(g + 1, 1 - slot, grp)

    qlat = qlat_ref[0]
    lat = lat_buf[slot].astype(BF16)
    kr = jnp.concatenate([kr_buf[slot, p] for p in range(pg)], axis=1).astype(BF16)
    ks_t = jnp.concatenate([ks_buf[slot, p] for p in range(pg)], axis=1)
    s = _dot_nt(qlat, lat) + _dot(qr_ref[0], kr)
    s = s * ks_t * ATTN_SCALE
    m_prev = m_sc[...]
    m_new = jnp.maximum(m_prev, jnp.max(s, axis=-1, keepdims=True))
    alpha = jnp.exp(m_prev - m_new)
    p = jnp.exp(s - m_new)
    l_sc[...] = alpha * l_sc[...] + jnp.sum(p, axis=-1, keepdims=True)
    acc_sc[...] = alpha * acc_sc[...] + _dot(p.astype(BF16), lat)
    m_sc[...] = m_new

    @pl.when(c == nch - 1)
    def _():
        ckv = ckv_ref[0]
        s_new = (jnp.sum(qlat.astype(F32) * ckv, axis=-1, keepdims=True)
                 + jnp.sum(qr_ref[0].astype(F32) * krn_ref[0], axis=-1, keepdims=True))
        s_new = s_new * ksn_ref[0] * ATTN_SCALE
        m_prev = m_sc[...]
        m_new = jnp.maximum(m_prev, s_new)
        alpha = jnp.exp(m_prev - m_new)
        p_new = jnp.exp(s_new - m_new)
        l = alpha * l_sc[...] + p_new
        o_ref[0] = (alpha * acc_sc[...] + p_new * ckv) / l


def _decode_attention(page_table, qlat, qr, ckv, krn, ksn, lat_pool, kr_pool, ks_pool, pg):
    B, n_pages = page_table.shape
    nch = n_pages // pg
    tk = pg * PAGE
    kern = functools.partial(_decode_kernel, pg=pg, nch=nch)
    per_b = lambda shape: pl.BlockSpec((1,) + shape, lambda b, c, pt: (b, 0, 0))
    grid_spec = pltpu.PrefetchScalarGridSpec(
        num_scalar_prefetch=1, grid=(B, nch),
        in_specs=[per_b((N_HEADS, KV_RANK)), per_b((N_HEADS, QK_ROPE)), per_b((1, KV_RANK)), per_b((1, QK_ROPE)),
                  per_b((N_HEADS, 1)),
                  pl.BlockSpec(memory_space=pl.ANY), pl.BlockSpec(memory_space=pl.ANY),
                  pl.BlockSpec(memory_space=pl.ANY)],
        out_specs=per_b((N_HEADS, KV_RANK)),
        scratch_shapes=[pltpu.VMEM((2, tk, KV_RANK), F32), pltpu.VMEM((2, pg, QK_ROPE, PAGE), F32),
                        pltpu.VMEM((2, pg, N_HEADS, PAGE), F32),
                        pltpu.SemaphoreType.DMA((2, 3, pl.cdiv(pg, PAGES_PER_SEM))),
                        pltpu.VMEM((N_HEADS, 1), F32), pltpu.VMEM((N_HEADS, 1), F32),
                        pltpu.VMEM((N_HEADS, KV_RANK), F32)])
    return pl.pallas_call(
        kern, grid_spec=grid_spec, out_shape=jax.ShapeDtypeStruct((B, N_HEADS, KV_RANK), F32),
        compiler_params=_cparams(("arbitrary", "arbitrary")), name="decode_attention",
    )(page_table.reshape(-1), qlat, qr, ckv, krn, ksn, lat_pool, kr_pool, ks_pool)


def _sample_attention(q, ckv, krope, kscale, page_table, lat_pool, kr_pool, ks_pool, W, pg):
    B = q.shape[0]
    qlat = pl.pallas_call(_qlat_kernel, out_shape=jax.ShapeDtypeStruct((B, N_HEADS * KV_RANK), BF16),
                          name="q_absorb")(q, W['g_kn_nope'], W['w_uk_t'])
    qr = q.reshape(B, N_HEADS, LANES)[:, :, QK_NOPE:QK_DIM].astype(BF16)
    ol = _decode_attention(page_table, qlat.reshape(B, N_HEADS, KV_RANK), qr, ckv.reshape(B, 1, KV_RANK),
                           krope.reshape(B, 1, QK_ROPE), kscale.reshape(B, N_HEADS, 1),
                           lat_pool, jnp.swapaxes(kr_pool, 1, 2), jnp.swapaxes(ks_pool, 1, 2), pg)
    return pl.pallas_call(_uv_kernel, out_shape=jax.ShapeDtypeStruct((B, HP), F32),
                          name="value_up")(ol.reshape(B, N_HEADS * KV_RANK), W['w_uv_p'])


def _mix_kernel(attn_ref, ssm_ref, x_ref, ga_ref, gs_ref, woa_ref, wos_ref, gffn_ref, wq_ref,
                x1_ref, xn_ref, qp_ref):
    a = _rms(attn_ref[...], ga_ref[...], D_ATTN).astype(BF16)
    s = _rms(ssm_ref[...], gs_ref[...], D_SSM).astype(BF16)
    x1 = x_ref[...] + _dot(a, woa_ref[...]) + _dot(s, wos_ref[...])
    x1_ref[...] = x1
    xn = _rms(x1, gffn_ref[...], D_MODEL).astype(BF16)
    xn_ref[...] = xn
    qp_ref[...] = _dot(xn, wq_ref[...]).astype(BF16)


def _mix(attn, ssm, x, W, tm):
    T = x.shape[0]
    tok = lambda w: pl.BlockSpec((tm, w), lambda i: (i, 0))
    weights = [W['g_attn_p'], W['g_ssm_out'], W['w_out_a'], W['w_out_s'], W['g_ffn'], W['w_peer_q']]
    nq = 2 * PEER_HEADS * PEER_HALF
    return pl.pallas_call(
        _mix_kernel, grid=(T // tm,),
        in_specs=[tok(HP), tok(D_SSM), tok(D_MODEL)] + [_full(w.shape) for w in weights],
        out_specs=[tok(D_MODEL), tok(D_MODEL), tok(nq)],
        out_shape=[jax.ShapeDtypeStruct((T, D_MODEL), F32), jax.ShapeDtypeStruct((T, D_MODEL), BF16),
                   jax.ShapeDtypeStruct((T, nq), BF16)],
        compiler_params=_cparams(("parallel",)), name="mix",
    )(attn, ssm, x, *weights)


_CAND = [(a, b) for a in range(PEER_TOPK) for b in range(PEER_TOPK) if (a + 1) * (b + 1) <= PEER_TOPK]


def _top16_rows(s):
    vals = []
    for _ in range(PEER_TOPK):
        m = jnp.max(s, axis=0, keepdims=True)
        vals.append(m)
        s = jnp.where(s == m, NEG, s)
    return vals


def _top16_ranked(s):
    vals = []
    rank = jnp.full(s.shape, float(PEER_TOPK), F32)
    for k in range(PEER_TOPK):
        m = jnp.max(s, axis=0, keepdims=True)
        vals.append(m)
        hit = s == m
        rank = jnp.where(hit, float(k), rank)
        s = jnp.where(hit, NEG, s)
    return vals, rank


def _route_kernel(qp_ref, keys_ref, n_ref, e1_ref, r2_ref, e2_ref):
    for h in range(PEER_HEADS):
        q1 = qp_ref[:, (2 * h) * PEER_HALF:(2 * h + 1) * PEER_HALF]
        q2 = qp_ref[:, (2 * h + 1) * PEER_HALF:(2 * h + 2) * PEER_HALF]
        s1 = _dot_nt(keys_ref[2 * h], q1)
        s2 = _dot_nt(keys_ref[2 * h + 1], q2)
        v1 = _top16_rows(s1)
        v2, r2 = _top16_ranked(s2)
        cand = [v1[a] + v2[b] for a, b in _CAND]
        call = jnp.concatenate(cand, axis=0)
        tau = _top16_rows(call)[PEER_TOPK - 1]
        cmax = cand[0]
        z = jnp.sum(jnp.where(call >= tau, jnp.exp(call - cmax), 0.0), axis=0, keepdims=True)
        n = jnp.zeros(s1.shape, F32)
        for k2 in range(PEER_TOPK):
            n = n + jnp.where(s1 + v2[k2] >= tau, 1.0, 0.0)
        n_ref[h] = n
        e1_ref[h] = jnp.exp(s1 - v1[0]) * (0.5 / z)
        r2_ref[h] = r2.astype(BF16)
        e2_ref[h] = jnp.exp(s2 - v2[0]).astype(BF16)


def _route(qp, keys, tt):
    T = qp.shape[0]
    big = lambda dt: jax.ShapeDtypeStruct((PEER_HEADS, N_KEYS, T), dt)
    bspec = pl.BlockSpec((PEER_HEADS, N_KEYS, tt), lambda i: (0, 0, i))
    return pl.pallas_call(
        _route_kernel, grid=(T // tt,),
        in_specs=[pl.BlockSpec((tt, qp.shape[1]), lambda i: (i, 0)), _full(keys.shape)],
        out_specs=[bspec, bspec, bspec, bspec],
        out_shape=[big(F32), big(F32), big(BF16), big(BF16)],
        compiler_params=_cparams(("parallel",)), name="peer_route",
    )(qp, keys)


ROWS_PER_STEP = 8
E_TILE = ROWS_PER_STEP * N_KEYS


def _peer_kernel(xn_ref, u_ref, vt_ref, n_ref, e1_ref, r2_ref, e2_ref, y_ref, acc_sc, at_sc):
    j = pl.program_id(1)

    @pl.when(j == 0)
    def _():
        acc_sc[...] = jnp.zeros(acc_sc.shape, F32)

    st = _dot_nt(u_ref[...], xn_ref[...])
    tt = st.shape[1]
    zero = jnp.zeros((), BF16)
    groups = N_KEYS // BF16_ROWS

    def row16(ref, h, r):
        return jnp.broadcast_to(ref[h, r:r + 1, :], (BF16_ROWS, tt)).astype(BF16)[None]

    for r in range(ROWS_PER_STEP):
        w = None
        for h in range(PEER_HEADS):
            r2 = r2_ref[h].reshape(groups, BF16_ROWS, tt)
            e2 = e2_ref[h].reshape(groups, BF16_ROWS, tt)
            g = jnp.where(r2 < row16(n_ref, h, r), e2 * row16(e1_ref, h, r), zero)
            w = g if w is None else w + g
        rows = slice(r * N_KEYS, (r + 1) * N_KEYS)
        at_sc[rows, :] = _gelu_x2(st[rows, :]).astype(BF16) * w.reshape(N_KEYS, tt)
    acc_sc[...] += _dot(vt_ref[...], at_sc[...])

    @pl.when(j == pl.num_programs(1) - 1)
    def _():
        y_ref[...] = acc_sc[...].T


def _peer_dense(xn, u_tab, vt_tab, n, e1, r2, e2, tt):
    T = xn.shape[0]
    rows = pl.BlockSpec((PEER_HEADS, ROWS_PER_STEP, tt), lambda i, j: (0, j, i))
    keys = pl.BlockSpec((PEER_HEADS, N_KEYS, tt), lambda i, j: (0, 0, i))
    return pl.pallas_call(
        _peer_kernel, grid=(T // tt, N_EXPERTS // E_TILE),
        in_specs=[pl.BlockSpec((tt, D_MODEL), lambda i, j: (i, 0)),
                  pl.BlockSpec((E_TILE, D_MODEL), lambda i, j: (j, 0)),
                  pl.BlockSpec((D_MODEL, E_TILE), lambda i, j: (0, j)),
                  rows, rows, keys, keys],
        out_specs=pl.BlockSpec((tt, D_MODEL), lambda i, j: (i, 0)),
        out_shape=jax.ShapeDtypeStruct((T, D_MODEL), F32),
        scratch_shapes=[pltpu.VMEM((D_MODEL, tt), F32), pltpu.VMEM((E_TILE, tt), BF16)],
        compiler_params=_cparams(("parallel", "arbitrary")), name="peer_experts",
    )(xn, u_tab, vt_tab, n, e1, r2, e2)


def _final_kernel(x1_ref, y_ref, p_ref, wple_ref, gple_ref, wgate_ref, o_ref):
    x2 = x1_ref[...] + y_ref[...]
    e = _dot(p_ref[...].astype(BF16), wple_ref[...])
    gate = _sigmoid(_dot(_rms(x2, gple_ref[...], D_MODEL).astype(BF16), wgate_ref[...]))
    o_ref[...] = x2 + e * gate


def _final(x1, y, p, W, tm):
    T = x1.shape[0]
    tok = lambda w: pl.BlockSpec((tm, w), lambda i: (i, 0))
    weights = [W['w_ple'], W['g_ple'], W['w_ple_gate']]
    return pl.pallas_call(
        _final_kernel, grid=(T // tm,),
        in_specs=[tok(D_MODEL), tok(D_MODEL), tok(PLE_DIM)] + [_full(w.shape) for w in weights],
        out_specs=tok(D_MODEL), out_shape=jax.ShapeDtypeStruct((T, D_MODEL), F32),
        compiler_params=_cparams(("parallel",)), name="ple_gate",
    )(x1, y, p, *weights)


def _pad_heads(w, width):
    pad = [(0, 0)] * (w.ndim - 1) + [(0, LANES - width)]
    w = jnp.pad(w, pad)
    return w.reshape(w.shape[:-2] + (HP,))


def _rope_tables(pos):
    inv = ROPE_THETA ** (-jnp.arange(0, QK_ROPE, 2, dtype=F32) / QK_ROPE)
    ang = pos.astype(F32)[:, None] * inv[None, :]
    cos, sin = jnp.cos(ang), jnp.sin(ang)
    T = pos.shape[0]
    cos_t = jnp.concatenate([jnp.ones((T, QK_NOPE), F32), cos, cos, jnp.zeros((T, LANES - QK_DIM), F32)], axis=1)
    sin_t = jnp.concatenate([jnp.zeros((T, QK_NOPE), F32), -sin, sin, jnp.zeros((T, LANES - QK_DIM), F32)], axis=1)
    return cos_t, sin_t


def _swap_rope(w):
    z = jnp.zeros_like(w[..., :QK_NOPE])
    return jnp.concatenate([z, w[..., 80:96], w[..., 64:80]], axis=-1)


def _prep_common(P):
    W = {}
    row = lambda g: g.reshape(1, -1).astype(F32)
    w_in = P['w_in']
    kr = w_in[:, 640:672]
    z64 = jnp.zeros((D_MODEL, QK_NOPE), F32)
    z32 = jnp.zeros((D_MODEL, LANES - QK_DIM), F32)
    W['w_in_p'] = jnp.concatenate([w_in[:, :640], w_in[:, 672:], z64, kr, z32,
                                   z64, kr[:, 16:], kr[:, :16], z32], axis=1).astype(BF16)
    W['g_mix'] = row(P['g_mix'])
    W['g_q_lat'] = row(P['g_q_lat'])
    W['w_uq_p'] = _pad_heads(P['w_uq'], QK_DIM).astype(BF16)
    W['w_uq_sw'] = _pad_heads(_swap_rope(P['w_uq']), QK_DIM).astype(BF16)
    g_qn = P['g_qn']
    W['g_qn_p'] = jnp.pad(g_qn, (0, LANES - QK_DIM)).reshape(1, LANES)
    W['g_qn_sw'] = jnp.pad(_swap_rope(g_qn), (0, LANES - QK_DIM)).reshape(1, LANES)
    W['g_kv_lat'] = row(P['g_kv_lat'])
    W['w_uk_p'] = _pad_heads(P['w_uk'], QK_NOPE).astype(BF16)
    W['w_uv_p'] = _pad_heads(P['w_uv'], V_DIM).astype(BF16)
    W['w_uk_t'] = jnp.pad(jnp.transpose(P['w_uk'], (1, 2, 0)),
                          ((0, 0), (0, LANES - QK_NOPE), (0, 0))).astype(BF16)
    g_kn = P['g_kn']
    W['g_kn_nope'] = jnp.pad(g_kn[:QK_NOPE], (0, LANES - QK_NOPE)).reshape(1, LANES)
    W['g_kc'] = jnp.pad(g_kn, (0, LANES - QK_DIM)).at[:QK_NOPE].set(0.0).reshape(1, LANES)
    W['g_ks'] = jnp.pad(_swap_rope(g_kn), (0, LANES - QK_DIM)).reshape(1, LANES)

    W['g_attn_p'] = _pad_heads(P['g_attn_out'].reshape(N_HEADS, V_DIM), V_DIM).reshape(1, HP)
    W['g_ssm_out'] = row(P['g_ssm_out'])
    w_out = P['w_out']
    W['w_out_a'] = jnp.pad(w_out[:D_ATTN].reshape(N_HEADS, V_DIM, D_MODEL),
                           ((0, 0), (0, LANES - V_DIM), (0, 0))).reshape(HP, D_MODEL).astype(BF16)
    W['w_out_s'] = w_out[D_ATTN:].astype(BF16)
    W['g_ffn'] = row(P['g_ffn'])
    W['w_peer_q'] = P['w_peer_q'].astype(BF16)
    W['peer_keys'] = P['peer_keys'].reshape(2 * PEER_HEADS, N_KEYS, PEER_HALF).astype(BF16)
    W['peer_u'] = P['peer_u'].astype(BF16)
    W['peer_vt'] = P['peer_v'].T.astype(BF16)
    W['w_ple'] = P['w_ple'].astype(BF16)
    W['g_ple'] = row(P['g_ple'])
    W['w_ple_gate'] = P['w_ple_gate'].astype(BF16)
    return W


def _ffn_tail(attn, ssm, x, p, W, tm, tt_route, tt_peer):
    x1, xn, qp = _mix(attn, ssm, x, W, tm)
    n, e1, r2, e2 = _route(qp, W['peer_keys'], tt_route)
    y = _peer_dense(xn, W['peer_u'], W['peer_vt'], n, e1, r2, e2, tt_peer)
    return _final(x1, y, p, W, tm)


_PARAM_NAMES = ('g_mix', 'w_in', 'g_q_lat', 'w_uq', 'g_kv_lat', 'w_uk', 'w_uv', 'g_qn', 'g_kn', 'ssm_a_re',
                'ssm_a_im', 'ssm_log_dt', 'ssm_b_re', 'ssm_b_im', 'ssm_c_re', 'ssm_c_im', 'ssm_d', 'w_glu', 'b_glu',
                'g_attn_out', 'g_ssm_out', 'w_out', 'g_ffn', 'w_peer_q', 'peer_keys', 'peer_u', 'peer_v', 'g_ple',
                'w_ple', 'w_ple_gate')

TM_TOKEN = 512
TQ_FLASH = 1024
TK_FLASH = 1024
TS_FLASH = 512
TB_SCAN = 512
TT_ROUTE = 256
TT_PEER = 1024
PAGES_PER_STEP = 32
PAGES_PER_SEM = 16
DMA_ISSUE_UNROLL = 4


def _layer(x_p, x_s, p_p, p_s, caches, states, page_table, P):
    W = _prep_common(P)
    W.update(_prep_ssm(P))
    T = x_p.shape[0]
    B = x_s.shape[0]
    n_past = page_table.shape[1] * PAGE

    cos_t, sin_t = _rope_tables(jnp.arange(T))
    q, k, v, ckv_p, kr_p, ks_p, u = _project(x_p, cos_t, sin_t, W, BF16, TM_TOKEN)
    attn = _flash_attention(q, k, v, TQ_FLASH, TK_FLASH, TS_FLASH)
    ssm, hre_p, him_p = _s5_prompt(u, W, TB_SCAN)
    y_p = _ffn_tail(attn, ssm, x_p, p_p, W, TM_TOKEN, TT_ROUTE, TT_PEER)

    cos_s, sin_s = _rope_tables(jnp.full((B,), n_past, jnp.int32))
    q, _, _, ckv_s, kr_s, ks_s, u = _project(x_s, cos_s, sin_s, W, F32, B)
    attn = _sample_attention(q, ckv_s, kr_s, ks_s, page_table, *caches, W, PAGES_PER_STEP)
    ssm, hre_s, him_s = _s5_step(u, states[0].reshape(B, N_STATE), states[1].reshape(B, N_STATE), W)
    y_s = _ffn_tail(attn, ssm, x_s, p_s, W, B, B, B)
    st = lambda a, lead: a.reshape(lead + (N_GROUPS, SSM_STATE))
    return (y_p, y_s, (ckv_p, kr_p, ks_p, st(hre_p, (1,)), st(him_p, (1,))),
            (ckv_s, kr_s, ks_s, st(hre_s, (B,)), st(him_s, (B,))))


def kernel(x_prompt, x_sample, cache_kv_latent, cache_k_rope, cache_k_scale, state_ssm_re, state_ssm_im,
           page_table, p_prompt, p_sample, g_mix, w_in, g_q_lat, w_uq, g_kv_lat, w_uk, w_uv, g_qn, g_kn,
           ssm_a_re, ssm_a_im, ssm_log_dt, ssm_b_re, ssm_b_im, ssm_c_re, ssm_c_im, ssm_d, w_glu, b_glu,
           g_attn_out, g_ssm_out, w_out, g_ffn, w_peer_q, peer_keys, peer_u, peer_v, g_ple, w_ple, w_ple_gate):
    params = dict(zip(_PARAM_NAMES, (g_mix, w_in, g_q_lat, w_uq, g_kv_lat, w_uk, w_uv, g_qn, g_kn, ssm_a_re,
                                     ssm_a_im, ssm_log_dt, ssm_b_re, ssm_b_im, ssm_c_re, ssm_c_im, ssm_d, w_glu,
                                     b_glu, g_attn_out, g_ssm_out, w_out, g_ffn, w_peer_q, peer_keys, peer_u,
                                     peer_v, g_ple, w_ple, w_ple_gate)))
    depth = g_mix.shape[0]
    assert x_prompt.shape[0] == 1 and x_sample.shape[1] == 1
    x_p = x_prompt[0]
    x_s = x_sample[:, 0]
    outs_p, outs_s = [], []
    for i in range(depth):
        P = {name: val[i] for name, val in params.items()}
        x_p, x_s, o_p, o_s = _layer(x_p, x_s, p_prompt[i, 0], p_sample[i, :, 0],
                                    (cache_kv_latent[i], cache_k_rope[i], cache_k_scale[i]),
                                    (state_ssm_re[i], state_ssm_im[i]), page_table, P)
        outs_p.append(o_p)
        outs_s.append(o_s)
    stack_p = [jnp.stack([o[j] for o in outs_p])[:, None] for j in range(5)]
    stack_s = [jnp.stack([o[j] for o in outs_s]) for j in range(5)]
    stack_s = [a[:, :, None] if j < 3 else a for j, a in enumerate(stack_s)]
    return (x_p[None], x_s[:, None], stack_p[0], stack_p[1], stack_p[2], stack_p[3][:, 0], stack_p[4][:, 0],
            stack_s[0], stack_s[1], stack_s[2], stack_s[3], stack_s[4])
```

```python
import functools
import math

import jax
import jax.numpy as jnp
import numpy as np
from jax import lax
from jax.experimental import pallas as pl
from jax.experimental.pallas import tpu as pltpu

F32 = jnp.float32
BF16 = jnp.bfloat16

D_MODEL = 1024
N_HEADS = 8
QK_NOPE = 64
QK_ROPE = 32
QK_DIM = QK_NOPE + QK_ROPE
V_DIM = 64
D_ATTN = N_HEADS * V_DIM
Q_RANK = 384
KV_RANK = 256
ROPE_THETA = 10000.0
ATTN_SCALE = QK_DIM ** -0.5
D_SSM = 512
SSM_GROUP = 16
N_GROUPS = 32
SSM_STATE = 64
N_STATE = N_GROUPS * SSM_STATE
PEER_HEADS = 8
N_KEYS = 128
N_EXPERTS = N_KEYS * N_KEYS
PEER_TOPK = 16
PEER_HALF = 128
PLE_DIM = 256
PAGE = 128
EPS = 1e-6

LANES = 128
BF16_ROWS = 16
HP = N_HEADS * LANES
NEG = -1e30
VMEM_LIMIT = 56 * 1024 * 1024


def _cparams(sem, vmem=VMEM_LIMIT):
    return pltpu.CompilerParams(dimension_semantics=sem, vmem_limit_bytes=vmem)


def _dot(a, b):
    return jnp.dot(a, b, preferred_element_type=F32)


def _dot_nt(a, b):
    return lax.dot_general(a, b, (((1,), (1,)), ((), ())), preferred_element_type=F32)


def _rms(x, g, n):
    ms = jnp.sum(x * x, axis=-1, keepdims=True) * (1.0 / n)
    return x * lax.rsqrt(ms + EPS) * g


def _gelu(x):
    return 0.5 * x * (1.0 + jnp.tanh(0.7978845608028654 * (x + 0.044715 * (x * x * x))))


def _gelu_x2(x):
    c0 = 0.7978845608028654
    t = jnp.tanh(x * (c0 + (c0 * 0.044715) * (x * x)))
    return x + x * t


def _sigmoid(x):
    return 1.0 / (1.0 + jnp.exp(-x))


def _full(shape):
    return pl.BlockSpec(shape, lambda *_: (0,) * len(shape))


def _proj_kernel(x_ref, cos_ref, sin_ref, gmix_ref, win_ref, gq_ref, wuq_ref, wuqs_ref, gqn_ref, gqns_ref,
                 gkv_ref, wuk_ref, wuv_ref, gknn_ref, gkc_ref, gks_ref,
                 q_ref, k_ref, v_ref, ckv_ref, krope_ref, kscale_ref, u_ref):
    x = x_ref[...]
    cos = cos_ref[...]
    sin = sin_ref[...]
    h = _rms(x, gmix_ref[...], D_MODEL).astype(BF16)
    proj = _dot(h, win_ref[...])
    cq = proj[:, :Q_RANK]
    ckv = proj[:, Q_RANK:Q_RANK + KV_RANK]
    u_ref[...] = proj[:, 640:1152]
    krt = proj[:, 1152:1280]
    krts = proj[:, 1280:1408]

    cqn = _rms(cq, gq_ref[...], Q_RANK).astype(BF16)
    q_raw = _dot(cqn, wuq_ref[...])
    q_sw = _dot(cqn, wuqs_ref[...])
    gqn = gqn_ref[...]
    gqns = gqns_ref[...]

    ckvn = _rms(ckv, gkv_ref[...], KV_RANK)
    ckv_ref[...] = ckvn
    ckvb = ckvn.astype(BF16)
    kn_raw = _dot(ckvb, wuk_ref[...])
    v_ref[...] = _dot(ckvb, wuv_ref[...]).astype(v_ref.dtype)
    ms_rope = jnp.sum(krt * krt, axis=-1, keepdims=True)
    kr_tile = (krt * gkc_ref[...]) * cos + (krts * gks_ref[...]) * sin
    krope_ref[...] = kr_tile[:, QK_NOPE:QK_DIM]
    gknn = gknn_ref[...]
    lane = lax.broadcasted_iota(jnp.int32, (1, LANES), 1)
    ks_all = jnp.zeros((x.shape[0], LANES), F32)
    for hd in range(N_HEADS):
        sl = slice(hd * LANES, (hd + 1) * LANES)
        qh = q_raw[:, sl]
        rq = lax.rsqrt(jnp.sum(qh * qh, axis=-1, keepdims=True) * (1.0 / QK_DIM) + EPS)
        q_ref[:, sl] = ((qh * rq * gqn) * cos + (q_sw[:, sl] * rq * gqns) * sin).astype(q_ref.dtype)
        kh = kn_raw[:, sl]
        ks = lax.rsqrt((jnp.sum(kh * kh, axis=-1, keepdims=True) + ms_rope) * (1.0 / QK_DIM) + EPS)
        k_ref[:, sl] = ((kh * gknn + kr_tile) * ks).astype(k_ref.dtype)
        ks_all = jnp.where(lane == hd, ks, ks_all)
    kscale_ref[...] = ks_all[:, :N_HEADS]


def _project(x, cos_t, sin_t, W, q_dtype, tm):
    T = x.shape[0]
    tok = lambda w: pl.BlockSpec((tm, w), lambda i: (i, 0))
    weights = [W['g_mix'], W['w_in_p'], W['g_q_lat'], W['w_uq_p'], W['w_uq_sw'], W['g_qn_p'], W['g_qn_sw'],
               W['g_kv_lat'], W['w_uk_p'], W['w_uv_p'], W['g_kn_nope'], W['g_kc'], W['g_ks']]
    out_shape = [jax.ShapeDtypeStruct((T, HP), q_dtype), jax.ShapeDtypeStruct((T, HP), BF16),
                 jax.ShapeDtypeStruct((T, HP), BF16), jax.ShapeDtypeStruct((T, KV_RANK), F32),
                 jax.ShapeDtypeStruct((T, QK_ROPE), F32), jax.ShapeDtypeStruct((T, N_HEADS), F32),
                 jax.ShapeDtypeStruct((T, D_SSM), F32)]
    return pl.pallas_call(
        _proj_kernel, grid=(T // tm,),
        in_specs=[tok(D_MODEL), tok(LANES), tok(LANES)] + [_full(w.shape) for w in weights],
        out_specs=[tok(HP), tok(HP), tok(HP), tok(KV_RANK), tok(QK_ROPE), tok(N_HEADS), tok(D_SSM)],
        out_shape=out_shape, compiler_params=_cparams(("parallel",)), name="project",
    )(x, cos_t, sin_t, *weights)


SCORE_SCALE = ATTN_SCALE * math.log2(math.e)


def _flash_kernel(q_ref, k_ref, v_ref, o_ref, m_sc, l_sc, acc_sc, *, tq, tk, ts):
    i = pl.program_id(1)
    nsub = tq // ts
    m_sc[...] = jnp.full(m_sc.shape, NEG, F32)
    l_sc[...] = jnp.zeros(l_sc.shape, F32)
    acc_sc[...] = jnp.zeros(acc_sc.shape, F32)
    def sub_step(a, col0, k, v, masked):
        rows = slice(a * ts, (a + 1) * ts)
        width = k.shape[0]
        s = _dot_nt(q_ref[rows, :], k)
        if masked:
            r = i * tq + a * ts + lax.broadcasted_iota(jnp.int32, (ts, width), 0)
            c = col0 + lax.broadcasted_iota(jnp.int32, (ts, width), 1)
            s = jnp.where(c <= r, s, NEG)
        m_prev = m_sc[rows, :]
        m_new = jnp.maximum(m_prev, jnp.max(s, axis=-1, keepdims=True))
        alpha = jnp.exp2((m_prev - m_new) * SCORE_SCALE)
        p = jnp.exp2((s - jnp.concatenate([m_new] * (width // LANES), axis=1)) * SCORE_SCALE)
        l_sc[rows, :] = alpha * l_sc[rows, :] + jnp.sum(p, axis=-1, keepdims=True)
        acc_sc[rows, :] = alpha * acc_sc[rows, :] + _dot(p.astype(BF16), v)
        m_sc[rows, :] = m_new

    def full_step(j, carry):
        start = pl.multiple_of(j * tk, tk)
        k = k_ref[pl.ds(start, tk), :]
        v = v_ref[pl.ds(start, tk), :]
        for a in range(nsub):
            sub_step(a, j * tk, k, v, False)
        return carry

    n_full = i * (tq // tk)
    lax.fori_loop(0, n_full, full_step, 0)
    for jj in range(nsub):
        start = pl.multiple_of(i * tq + jj * ts, ts)
        k = k_ref[pl.ds(start, ts), :]
        v = v_ref[pl.ds(start, ts), :]
        for a in range(jj, nsub):
            sub_step(a, i * tq + jj * ts, k, v, a == jj)
    o_ref[...] = acc_sc[...] / l_sc[...]


def _flash_attention(q, k, v, tq, tk, ts):
    T = q.shape[0]
    kern = functools.partial(_flash_kernel, tq=tq, tk=tk, ts=ts)
    return pl.pallas_call(
        kern, grid=(N_HEADS, T // tq),
        in_specs=[pl.BlockSpec((tq, LANES), lambda h, i: (i, h)),
                  pl.BlockSpec((T, LANES), lambda h, i: (0, h)),
                  pl.BlockSpec((T, LANES), lambda h, i: (0, h))],
        out_specs=pl.BlockSpec((tq, LANES), lambda h, i: (i, h)),
        out_shape=jax.ShapeDtypeStruct((T, HP), F32),
        scratch_shapes=[pltpu.VMEM((tq, LANES), F32), pltpu.VMEM((tq, LANES), F32), pltpu.VMEM((tq, LANES), F32)],
        compiler_params=_cparams(("parallel", "arbitrary")), name="flash_attention",
    )(q, k, v)


def _cmul(ar, ai, br, bi):
    return ar * br - ai * bi, ar * bi + ai * br


def _ssm_disc_kernel(are_ref, aim_ref, dt_ref, bre_ref, bim_ref, abr_ref, abi_ref, bbr_ref, bbi_ref):
    lr = are_ref[...]
    li = aim_ref[...]
    dt = dt_ref[...]
    mag = jnp.exp(lr * dt)
    abr = mag * jnp.cos(li * dt)
    abi = mag * jnp.sin(li * dt)
    abr_ref[...] = abr
    abi_ref[...] = abi
    inv = 1.0 / (lr * lr + li * li)
    cr, ci = _cmul(abr - 1.0, abi, lr * inv, -li * inv)
    bbr, bbi = _cmul(cr, ci, bre_ref[...], bim_ref[...])
    bbr_ref[...] = bbr
    bbi_ref[...] = bbi


def _ssm_discretise(P):
    col = lambda a: a.reshape(N_STATE, 1)
    dt = jnp.repeat(jnp.exp(P['ssm_log_dt']), SSM_STATE).reshape(N_STATE, 1)
    bre = P['ssm_b_re'].reshape(N_STATE, SSM_GROUP)
    bim = P['ssm_b_im'].reshape(N_STATE, SSM_GROUP)
    s1 = jax.ShapeDtypeStruct((N_STATE, 1), F32)
    s16 = jax.ShapeDtypeStruct((N_STATE, SSM_GROUP), F32)
    return pl.pallas_call(_ssm_disc_kernel, out_shape=[s1, s1, s16, s16], name="ssm_discretise")(
        col(P['ssm_a_re']), col(P['ssm_a_im']), dt, bre, bim)


def _prep_ssm(P):
    abr, abi, bbr, bbi = _ssm_discretise(P)
    eye = jnp.eye(N_GROUPS, dtype=F32)

    def bdiag_in(b):
        b = b.reshape(N_GROUPS, SSM_STATE, SSM_GROUP)
        return jnp.einsum('gpc,gh->gchp', b, eye).reshape(D_SSM, N_STATE)

    def bdiag_out(c):
        return jnp.einsum('gcp,gh->gphc', c, eye).reshape(N_STATE, D_SSM)

    W = {}
    W['ssm_bm'] = jnp.concatenate([bdiag_in(bbr), bdiag_in(bbi)], axis=1)
    W['ssm_cm'] = jnp.concatenate([bdiag_out(P['ssm_c_re']), -bdiag_out(P['ssm_c_im'])], axis=0)
    W['ssm_ar'] = abr.reshape(1, N_STATE)
    W['ssm_ai'] = abi.reshape(1, N_STATE)
    W['ssm_d'] = P['ssm_d'].reshape(1, D_SSM)
    W['w_glu'] = P['w_glu'].astype(BF16)
    W['b_glu'] = P['b_glu'].reshape(1, D_SSM)
    return W


SCAN_LANES = 512
SCAN_UNROLL = 4


def _s5_prompt_kernel(u_ref, perm_ref, permt_ref, bm_ref, cm_ref, ar_ref, ai_ref, d_ref, wglu_ref, bglu_ref,
                      out_ref, hre_ref, him_ref, x_sc, carry_sc, *, tb):
    n = tb // 8
    nk = N_STATE // SCAN_LANES
    step = pl.program_id(0)

    @pl.when(step == 0)
    def _():
        carry_sc[...] = jnp.zeros(carry_sc.shape, F32)

    u = u_ref[...]
    up = _dot(perm_ref[...], u.astype(BF16)).astype(BF16)
    for k in range(nk):
        uk = up[:, k * LANES:(k + 1) * LANES]
        for c in range(2):
            lo = c * N_STATE + k * SCAN_LANES
            x_sc[:, lo:lo + SCAN_LANES] = _dot(uk, bm_ref[k * LANES:(k + 1) * LANES, lo:lo + SCAN_LANES])

    for k in range(nk):
        re_sl = slice(k * SCAN_LANES, (k + 1) * SCAN_LANES)
        im_sl = slice(N_STATE + k * SCAN_LANES, N_STATE + (k + 1) * SCAN_LANES)
        ar = jnp.broadcast_to(ar_ref[:, re_sl], (8, SCAN_LANES))
        ai = jnp.broadcast_to(ai_ref[:, re_sl], (8, SCAN_LANES))

        def local_scan(i, h):
            rows = pl.ds(pl.multiple_of(i * 8, 8), 8)
            pr, pi = _cmul(ar, ai, h[0], h[1])
            hr = pr + x_sc[rows, re_sl]
            hi = pi + x_sc[rows, im_sl]
            x_sc[rows, re_sl] = hr
            x_sc[rows, im_sl] = hi
            return hr, hi

        zero = jnp.zeros((8, SCAN_LANES), F32)
        er, ei = lax.fori_loop(0, n, local_scan, (zero, zero), unroll=SCAN_UNROLL)

        pr, pi = ar, ai
        for _ in range(int(math.log2(n))):
            pr, pi = _cmul(pr, pi, pr, pi)
        cr = [carry_sc[0:1, re_sl]]
        ci = [carry_sc[1:2, re_sl]]
        for s in range(1, 9):
            qr, qi = _cmul(pr[0:1], pi[0:1], cr[-1], ci[-1])
            cr.append(qr + er[s - 1:s])
            ci.append(qi + ei[s - 1:s])
        carry_sc[0:1, re_sl] = cr[8]
        carry_sc[1:2, re_sl] = ci[8]
        cin = (jnp.concatenate(cr[:8], axis=0), jnp.concatenate(ci[:8], axis=0))

        def patch(i, c):
            rows = pl.ds(pl.multiple_of(i * 8, 8), 8)
            c = _cmul(ar, ai, c[0], c[1])
            x_sc[rows, re_sl] = x_sc[rows, re_sl] + c[0]
            x_sc[rows, im_sl] = x_sc[rows, im_sl] + c[1]
            return c

        lax.fori_loop(0, n, patch, cin, unroll=SCAN_UNROLL)

    ys = []
    for k in range(nk):
        zr = x_sc[:, k * SCAN_LANES:(k + 1) * SCAN_LANES].astype(BF16)
        zi = x_sc[:, N_STATE + k * SCAN_LANES:N_STATE + (k + 1) * SCAN_LANES].astype(BF16)
        ys.append(_dot(zr, cm_ref[k * SCAN_LANES:(k + 1) * SCAN_LANES, k * LANES:(k + 1) * LANES])
                  + _dot(zi, cm_ref[N_STATE + k * SCAN_LANES:N_STATE + (k + 1) * SCAN_LANES,
                                    k * LANES:(k + 1) * LANES]))
    yp = jnp.concatenate(ys, axis=1)
    pt = permt_ref[...]
    hi = yp.astype(BF16)
    r1 = yp - hi.astype(F32)
    mid = r1.astype(BF16)
    lo = (r1 - mid.astype(F32)).astype(BF16)
    y = _dot(pt, hi) + _dot(pt, mid) + _dot(pt, lo) + d_ref[...] * u
    g = _gelu(y)
    out_ref[...] = g * _sigmoid(_dot(g.astype(BF16), wglu_ref[...]) + bglu_ref[...])
    hre_ref[...] = carry_sc[0:1, :]
    him_ref[...] = carry_sc[1:2, :]


def _s5_prompt(u, W, tb):
    T = u.shape[0]
    n = tb // 8
    r = np.arange(tb)
    src = (r % 8) * n + r // 8
    perm = np.zeros((tb, tb), np.float32)
    perm[r, src] = 1.0
    kern = functools.partial(_s5_prompt_kernel, tb=tb)
    weights = [jnp.asarray(perm, BF16), jnp.asarray(perm.T, BF16), W['ssm_bm'].astype(BF16),
               W['ssm_cm'].astype(BF16), W['ssm_ar'], W['ssm_ai'], W['ssm_d'], W['w_glu'], W['b_glu']]
    return pl.pallas_call(
        kern, grid=(T // tb,),
        in_specs=[pl.BlockSpec((tb, D_SSM), lambda i: (i, 0))] + [_full(w.shape) for w in weights],
        out_specs=[pl.BlockSpec((tb, D_SSM), lambda i: (i, 0)), _full((1, N_STATE)), _full((1, N_STATE))],
        out_shape=[jax.ShapeDtypeStruct((T, D_SSM), F32), jax.ShapeDtypeStruct((1, N_STATE), F32),
                   jax.ShapeDtypeStruct((1, N_STATE), F32)],
        scratch_shapes=[pltpu.VMEM((tb, 2 * N_STATE), F32), pltpu.VMEM((2, N_STATE), F32)],
        compiler_params=_cparams(("arbitrary",)), name="s5_prompt",
    )(u, *weights)


def _s5_step_kernel(u_ref, hre_ref, him_ref, bm_ref, cm_ref, ar_ref, ai_ref, d_ref, wglu_ref, bglu_ref,
                    out_ref, nre_ref, nim_ref):
    u = u_ref[...]
    bm = bm_ref[...]
    uh = u.astype(BF16)
    ul = (u - uh.astype(F32)).astype(BF16)
    bh = bm.astype(BF16)
    bl = (bm - bh.astype(F32)).astype(BF16)
    bu = _dot(uh, bh) + _dot(uh, bl) + _dot(ul, bh)
    pr, pi = _cmul(ar_ref[...], ai_ref[...], hre_ref[...], him_ref[...])
    hr = pr + bu[:, :N_STATE]
    hi = pi + bu[:, N_STATE:]
    nre_ref[...] = hr
    nim_ref[...] = hi
    cm = cm_ref[...]
    y = _dot(hr.astype(BF16), cm[:N_STATE]) + _dot(hi.astype(BF16), cm[N_STATE:]) + d_ref[...] * u
    g = _gelu(y)
    out_ref[...] = g * _sigmoid(_dot(g.astype(BF16), wglu_ref[...]) + bglu_ref[...])


def _s5_step(u, hre, him, W):
    B = u.shape[0]
    st = jax.ShapeDtypeStruct((B, N_STATE), F32)
    return pl.pallas_call(
        _s5_step_kernel, out_shape=[jax.ShapeDtypeStruct((B, D_SSM), F32), st, st],
        compiler_params=pltpu.CompilerParams(vmem_limit_bytes=VMEM_LIMIT), name="s5_step",
    )(u, hre, him, W['ssm_bm'], W['ssm_cm'].astype(BF16), W['ssm_ar'], W['ssm_ai'], W['ssm_d'],
      W['w_glu'], W['b_glu'])


def _qlat_kernel(q_ref, gknn_ref, wukt_ref, o_ref):
    g = gknn_ref[...]
    for h in range(N_HEADS):
        qh = (q_ref[:, h * LANES:(h + 1) * LANES] * g).astype(BF16)
        o_ref[:, h * KV_RANK:(h + 1) * KV_RANK] = _dot(qh, wukt_ref[h]).astype(o_ref.dtype)


def _uv_kernel(ol_ref, wuv_ref, o_ref):
    for h in range(N_HEADS):
        o_ref[:, h * LANES:(h + 1) * LANES] = _dot(ol_ref[:, h * KV_RANK:(h + 1) * KV_RANK].astype(BF16),
                                                   wuv_ref[:, h * LANES:(h + 1) * LANES])


def _decode_kernel(pt_ref, qlat_ref, qr_ref, ckv_ref, krn_ref, ksn_ref, lat_hbm, kr_hbm, ks_hbm, o_ref,
                   lat_buf, kr_buf, ks_buf, sem, m_sc, l_sc, acc_sc, *, pg, nch):
    b = pl.program_id(0)
    c = pl.program_id(1)
    g = b * nch + c
    total = pl.num_programs(0) * nch

    def copies(step, slot, p):
        page = pt_ref[step * pg + p]
        grp = p // PAGES_PER_SEM
        return (pltpu.make_async_copy(lat_hbm.at[page], lat_buf.at[slot, pl.ds(p * PAGE, PAGE)],
                                      sem.at[slot, 0, grp]),
                pltpu.make_async_copy(kr_hbm.at[page], kr_buf.at[slot, p], sem.at[slot, 1, grp]),
                pltpu.make_async_copy(ks_hbm.at[page], ks_buf.at[slot, p], sem.at[slot, 2, grp]))

    n_groups = pl.cdiv(pg, PAGES_PER_SEM)

    def group_pages(grp):
        first = grp * PAGES_PER_SEM
        return first, min(PAGES_PER_SEM, pg - first)

    def start_group(step, slot, grp):
        first, count = group_pages(grp)

        def body(p, carry):
            for d in copies(step, slot, p):
                d.start()
            return carry
        lax.fori_loop(first, first + count, body, 0, unroll=DMA_ISSUE_UNROLL)

    def wait_group(slot, grp):
        first, count = group_pages(grp)
        lat_grp = lat_buf.at[slot, pl.ds(first * PAGE, count * PAGE)]
        kr_grp = kr_buf.at[slot, pl.ds(first, count)]
        ks_grp = ks_buf.at[slot, pl.ds(first, count)]
        pltpu.make_async_copy(lat_grp, lat_grp, sem.at[slot, 0, grp]).wait()
        pltpu.make_async_copy(kr_grp, kr_grp, sem.at[slot, 1, grp]).wait()
        pltpu.make_async_copy(ks_grp, ks_grp, sem.at[slot, 2, grp]).wait()

    @pl.when(g == 0)
    def _():
        for grp in range(n_groups):
            start_group(0, 0, grp)

    @pl.when(c == 0)
    def _():
        m_sc[...] = jnp.full(m_sc.shape, NEG, F32)
        l_sc[...] = jnp.zeros(l_sc.shape, F32)
        acc_sc[...] = jnp.zeros(acc_sc.shape, F32)

    slot = g % 2
    for grp in range(n_groups):
        wait_group(slot, grp)

        @pl.when(g + 1 < total)
        def _():
            start_group(g + 1, 1 - slot, grp)

    qlat = qlat_ref[0]
    lat = lat_buf[slot].astype(BF16)
    kr = jnp.concatenate([kr_buf[slot, p] for p in range(pg)], axis=1).astype(BF16)
    ks_t = jnp.concatenate([ks_buf[slot, p] for p in range(pg)], axis=1)
    s = _dot_nt(qlat, lat) + _dot(qr_ref[0], kr)
    s = s * ks_t * ATTN_SCALE
    m_prev = m_sc[...]
    m_new = jnp.maximum(m_prev, jnp.max(s, axis=-1, keepdims=True))
    alpha = jnp.exp(m_prev - m_new)
    p = jnp.exp(s - m_new)
    l_sc[...] = alpha * l_sc[...] + jnp.sum(p, axis=-1, keepdims=True)
    acc_sc[...] = alpha * acc_sc[...] + _dot(p.astype(BF16), lat)
    m_sc[...] = m_new

    @pl.when(c == nch - 1)
    def _():
        ckv = ckv_ref[0]
        s_new = (jnp.sum(qlat.astype(F32) * ckv, axis=-1, keepdims=True)
                 + jnp.sum(qr_ref[0].astype(F32) * krn_ref[0], axis=-1, keepdims=True))
        s_new = s_new * ksn_ref[0] * ATTN_SCALE
        m_prev = m_sc[...]
        m_new = jnp.maximum(m_prev, s_new)
        alpha = jnp.exp(m_prev - m_new)
        p_new = jnp.exp(s_new - m_new)
        l = alpha * l_sc[...] + p_new
        o_ref[0] = (alpha * acc_sc[...] + p_new * ckv) / l


def _decode_attention(page_table, qlat, qr, ckv, krn, ksn, lat_pool, kr_pool, ks_pool, pg):
    B, n_pages = page_table.shape
    nch = n_pages // pg
    tk = pg * PAGE
    kern = functools.partial(_decode_kernel, pg=pg, nch=nch)
    per_b = lambda shape: pl.BlockSpec((1,) + shape, lambda b, c, pt: (b, 0, 0))
    grid_spec = pltpu.PrefetchScalarGridSpec(
        num_scalar_prefetch=1, grid=(B, nch),
        in_specs=[per_b((N_HEADS, KV_RANK)), per_b((N_HEADS, QK_ROPE)), per_b((1, KV_RANK)), per_b((1, QK_ROPE)),
                  per_b((N_HEADS, 1)),
                  pl.BlockSpec(memory_space=pl.ANY), pl.BlockSpec(memory_space=pl.ANY),
                  pl.BlockSpec(memory_space=pl.ANY)],
        out_specs=per_b((N_HEADS, KV_RANK)),
        scratch_shapes=[pltpu.VMEM((2, tk, KV_RANK), F32), pltpu.VMEM((2, pg, QK_ROPE, PAGE), F32),
                        pltpu.VMEM((2, pg, N_HEADS, PAGE), F32),
                        pltpu.SemaphoreType.DMA((2, 3, pl.cdiv(pg, PAGES_PER_SEM))),
                        pltpu.VMEM((N_HEADS, 1), F32), pltpu.VMEM((N_HEADS, 1), F32),
                        pltpu.VMEM((N_HEADS, KV_RANK), F32)])
    return pl.pallas_call(
        kern, grid_spec=grid_spec, out_shape=jax.ShapeDtypeStruct((B, N_HEADS, KV_RANK), F32),
        compiler_params=_cparams(("arbitrary", "arbitrary")), name="decode_attention",
    )(page_table.reshape(-1), qlat, qr, ckv, krn, ksn, lat_pool, kr_pool, ks_pool)


def _sample_attention(q, ckv, krope, kscale, page_table, lat_pool, kr_pool, ks_pool, W, pg):
    B = q.shape[0]
    qlat = pl.pallas_call(_qlat_kernel, out_shape=jax.ShapeDtypeStruct((B, N_HEADS * KV_RANK), BF16),
                          name="q_absorb")(q, W['g_kn_nope'], W['w_uk_t'])
    qr = q.reshape(B, N_HEADS, LANES)[:, :, QK_NOPE:QK_DIM].astype(BF16)
    ol = _decode_attention(page_table, qlat.reshape(B, N_HEADS, KV_RANK), qr, ckv.reshape(B, 1, KV_RANK),
                           krope.reshape(B, 1, QK_ROPE), kscale.reshape(B, N_HEADS, 1),
                           lat_pool, jnp.swapaxes(kr_pool, 1, 2), jnp.swapaxes(ks_pool, 1, 2), pg)
    return pl.pallas_call(_uv_kernel, out_shape=jax.ShapeDtypeStruct((B, HP), F32),
                          name="value_up")(ol.reshape(B, N_HEADS * KV_RANK), W['w_uv_p'])


def _mix_kernel(attn_ref, ssm_ref, x_ref, ga_ref, gs_ref, woa_ref, wos_ref, gffn_ref, wq_ref,
                x1_ref, xn_ref, qp_ref):
    a = _rms(attn_ref[...], ga_ref[...], D_ATTN).astype(BF16)
    s = _rms(ssm_ref[...], gs_ref[...], D_SSM).astype(BF16)
    x1 = x_ref[...] + _dot(a, woa_ref[...]) + _dot(s, wos_ref[...])
    x1_ref[...] = x1
    xn = _rms(x1, gffn_ref[...], D_MODEL).astype(BF16)
    xn_ref[...] = xn
    qp_ref[...] = _dot(xn, wq_ref[...]).astype(BF16)


def _mix(attn, ssm, x, W, tm):
    T = x.shape[0]
    tok = lambda w: pl.BlockSpec((tm, w), lambda i: (i, 0))
    weights = [W['g_attn_p'], W['g_ssm_out'], W['w_out_a'], W['w_out_s'], W['g_ffn'], W['w_peer_q']]
    nq = 2 * PEER_HEADS * PEER_HALF
    return pl.pallas_call(
        _mix_kernel, grid=(T // tm,),
        in_specs=[tok(HP), tok(D_SSM), tok(D_MODEL)] + [_full(w.shape) for w in weights],
        out_specs=[tok(D_MODEL), tok(D_MODEL), tok(nq)],
        out_shape=[jax.ShapeDtypeStruct((T, D_MODEL), F32), jax.ShapeDtypeStruct((T, D_MODEL), BF16),
                   jax.ShapeDtypeStruct((T, nq), BF16)],
        compiler_params=_cparams(("parallel",)), name="mix",
    )(attn, ssm, x, *weights)


_CAND = [(a, b) for a in range(PEER_TOPK) for b in range(PEER_TOPK) if (a + 1) * (b + 1) <= PEER_TOPK]


def _top16_rows(s):
    vals = []
    for _ in range(PEER_TOPK):
        m = jnp.max(s, axis=0, keepdims=True)
        vals.append(m)
        s = jnp.where(s == m, NEG, s)
    return vals


def _top16_ranked(s):
    vals = []
    rank = jnp.full(s.shape, float(PEER_TOPK), F32)
    for k in range(PEER_TOPK):
        m = jnp.max(s, axis=0, keepdims=True)
        vals.append(m)
        hit = s == m
        rank = jnp.where(hit, float(k), rank)
        s = jnp.where(hit, NEG, s)
    return vals, rank


def _route_kernel(qp_ref, keys_ref, n_ref, e1_ref, r2_ref, e2_ref):
    for h in range(PEER_HEADS):
        q1 = qp_ref[:, (2 * h) * PEER_HALF:(2 * h + 1) * PEER_HALF]
        q2 = qp_ref[:, (2 * h + 1) * PEER_HALF:(2 * h + 2) * PEER_HALF]
        s1 = _dot_nt(keys_ref[2 * h], q1)
        s2 = _dot_nt(keys_ref[2 * h + 1], q2)
        v1 = _top16_rows(s1)
        v2, r2 = _top16_ranked(s2)
        cand = [v1[a] + v2[b] for a, b in _CAND]
        call = jnp.concatenate(cand, axis=0)
        tau = _top16_rows(call)[PEER_TOPK - 1]
        cmax = cand[0]
        z = jnp.sum(jnp.where(call >= tau, jnp.exp(call - cmax), 0.0), axis=0, keepdims=True)
        n = jnp.zeros(s1.shape, F32)
        for k2 in range(PEER_TOPK):
            n = n + jnp.where(s1 + v2[k2] >= tau, 1.0, 0.0)
        n_ref[h] = n
        e1_ref[h] = jnp.exp(s1 - v1[0]) * (0.5 / z)
        r2_ref[h] = r2.astype(BF16)
        e2_ref[h] = jnp.exp(s2 - v2[0]).astype(BF16)


def _route(qp, keys, tt):
    T = qp.shape[0]
    big = lambda dt: jax.ShapeDtypeStruct((PEER_HEADS, N_KEYS, T), dt)
    bspec = pl.BlockSpec((PEER_HEADS, N_KEYS, tt), lambda i: (0, 0, i))
    return pl.pallas_call(
        _route_kernel, grid=(T // tt,),
        in_specs=[pl.BlockSpec((tt, qp.shape[1]), lambda i: (i, 0)), _full(keys.shape)],
        out_specs=[bspec, bspec, bspec, bspec],
        out_shape=[big(F32), big(F32), big(BF16), big(BF16)],
        compiler_params=_cparams(("parallel",)), name="peer_route",
    )(qp, keys)


ROWS_PER_STEP = 8
E_TILE = ROWS_PER_STEP * N_KEYS


def _peer_kernel(xn_ref, u_ref, vt_ref, n_ref, e1_ref, r2_ref, e2_ref, y_ref, acc_sc, at_sc):
    j = pl.program_id(1)

    @pl.when(j == 0)
    def _():
        acc_sc[...] = jnp.zeros(acc_sc.shape, F32)

    st = _dot_nt(u_ref[...], xn_ref[...])
    tt = st.shape[1]
    zero = jnp.zeros((), BF16)
    groups = N_KEYS // BF16_ROWS

    def row16(ref, h, r):
        return jnp.broadcast_to(ref[h, r:r + 1, :], (BF16_ROWS, tt)).astype(BF16)[None]

    for r in range(ROWS_PER_STEP):
        w = None
        for h in range(PEER_HEADS):
            r2 = r2_ref[h].reshape(groups, BF16_ROWS, tt)
            e2 = e2_ref[h].reshape(groups, BF16_ROWS, tt)
            g = jnp.where(r2 < row16(n_ref, h, r), e2 * row16(e1_ref, h, r), zero)
            w = g if w is None else w + g
        rows = slice(r * N_KEYS, (r + 1) * N_KEYS)
        at_sc[rows, :] = _gelu_x2(st[rows, :]).astype(BF16) * w.reshape(N_KEYS, tt)
    acc_sc[...] += _dot(vt_ref[...], at_sc[...])

    @pl.when(j == pl.num_programs(1) - 1)
    def _():
        y_ref[...] = acc_sc[...].T


def _peer_dense(xn, u_tab, vt_tab, n, e1, r2, e2, tt):
    T = xn.shape[0]
    rows = pl.BlockSpec((PEER_HEADS, ROWS_PER_STEP, tt), lambda i, j: (0, j, i))
    keys = pl.BlockSpec((PEER_HEADS, N_KEYS, tt), lambda i, j: (0, 0, i))
    return pl.pallas_call(
        _peer_kernel, grid=(T // tt, N_EXPERTS // E_TILE),
        in_specs=[pl.BlockSpec((tt, D_MODEL), lambda i, j: (i, 0)),
                  pl.BlockSpec((E_TILE, D_MODEL), lambda i, j: (j, 0)),
                  pl.BlockSpec((D_MODEL, E_TILE), lambda i, j: (0, j)),
                  rows, rows, keys, keys],
        out_specs=pl.BlockSpec((tt, D_MODEL), lambda i, j: (i, 0)),
        out_shape=jax.ShapeDtypeStruct((T, D_MODEL), F32),
        scratch_shapes=[pltpu.VMEM((D_MODEL, tt), F32), pltpu.VMEM((E_TILE, tt), BF16)],
        compiler_params=_cparams(("parallel", "arbitrary")), name="peer_experts",
    )(xn, u_tab, vt_tab, n, e1, r2, e2)


def _final_kernel(x1_ref, y_ref, p_ref, wple_ref, gple_ref, wgate_ref, o_ref):
    x2 = x1_ref[...] + y_ref[...]
    e = _dot(p_ref[...].astype(BF16), wple_ref[...])
    gate = _sigmoid(_dot(_rms(x2, gple_ref[...], D_MODEL).astype(BF16), wgate_ref[...]))
    o_ref[...] = x2 + e * gate


def _final(x1, y, p, W, tm):
    T = x1.shape[0]
    tok = lambda w: pl.BlockSpec((tm, w), lambda i: (i, 0))
    weights = [W['w_ple'], W['g_ple'], W['w_ple_gate']]
    return pl.pallas_call(
        _final_kernel, grid=(T // tm,),
        in_specs=[tok(D_MODEL), tok(D_MODEL), tok(PLE_DIM)] + [_full(w.shape) for w in weights],
        out_specs=tok(D_MODEL), out_shape=jax.ShapeDtypeStruct((T, D_MODEL), F32),
        compiler_params=_cparams(("parallel",)), name="ple_gate",
    )(x1, y, p, *weights)


def _pad_heads(w, width):
    pad = [(0, 0)] * (w.ndim - 1) + [(0, LANES - width)]
    w = jnp.pad(w, pad)
    return w.reshape(w.shape[:-2] + (HP,))


def _rope_tables(pos):
    inv = ROPE_THETA ** (-jnp.arange(0, QK_ROPE, 2, dtype=F32) / QK_ROPE)
    ang = pos.astype(F32)[:, None] * inv[None, :]
    cos, sin = jnp.cos(ang), jnp.sin(ang)
    T = pos.shape[0]
    cos_t = jnp.concatenate([jnp.ones((T, QK_NOPE), F32), cos, cos, jnp.zeros((T, LANES - QK_DIM), F32)], axis=1)
    sin_t = jnp.concatenate([jnp.zeros((T, QK_NOPE), F32), -sin, sin, jnp.zeros((T, LANES - QK_DIM), F32)], axis=1)
    return cos_t, sin_t


def _swap_rope(w):
    z = jnp.zeros_like(w[..., :QK_NOPE])
    return jnp.concatenate([z, w[..., 80:96], w[..., 64:80]], axis=-1)


def _prep_common(P):
    W = {}
    row = lambda g: g.reshape(1, -1).astype(F32)
    w_in = P['w_in']
    kr = w_in[:, 640:672]
    z64 = jnp.zeros((D_MODEL, QK_NOPE), F32)
    z32 = jnp.zeros((D_MODEL, LANES - QK_DIM), F32)
    W['w_in_p'] = jnp.concatenate([w_in[:, :640], w_in[:, 672:], z64, kr, z32,
                                   z64, kr[:, 16:], kr[:, :16], z32], axis=1).astype(BF16)
    W['g_mix'] = row(P['g_mix'])
    W['g_q_lat'] = row(P['g_q_lat'])
    W['w_uq_p'] = _pad_heads(P['w_uq'], QK_DIM).astype(BF16)
    W['w_uq_sw'] = _pad_heads(_swap_rope(P['w_uq']), QK_DIM).astype(BF16)
    g_qn = P['g_qn']
    W['g_qn_p'] = jnp.pad(g_qn, (0, LANES - QK_DIM)).reshape(1, LANES)
    W['g_qn_sw'] = jnp.pad(_swap_rope(g_qn), (0, LANES - QK_DIM)).reshape(1, LANES)
    W['g_kv_lat'] = row(P['g_kv_lat'])
    W['w_uk_p'] = _pad_heads(P['w_uk'], QK_NOPE).astype(BF16)
    W['w_uv_p'] = _pad_heads(P['w_uv'], V_DIM).astype(BF16)
    W['w_uk_t'] = jnp.pad(jnp.transpose(P['w_uk'], (1, 2, 0)),
                          ((0, 0), (0, LANES - QK_NOPE), (0, 0))).astype(BF16)
    g_kn = P['g_kn']
    W['g_kn_nope'] = jnp.pad(g_kn[:QK_NOPE], (0, LANES - QK_NOPE)).reshape(1, LANES)
    W['g_kc'] = jnp.pad(g_kn, (0, LANES - QK_DIM)).at[:QK_NOPE].set(0.0).reshape(1, LANES)
    W['g_ks'] = jnp.pad(_swap_rope(g_kn), (0, LANES - QK_DIM)).reshape(1, LANES)

    W['g_attn_p'] = _pad_heads(P['g_attn_out'].reshape(N_HEADS, V_DIM), V_DIM).reshape(1, HP)
    W['g_ssm_out'] = row(P['g_ssm_out'])
    w_out = P['w_out']
    W['w_out_a'] = jnp.pad(w_out[:D_ATTN].reshape(N_HEADS, V_DIM, D_MODEL),
                           ((0, 0), (0, LANES - V_DIM), (0, 0))).reshape(HP, D_MODEL).astype(BF16)
    W['w_out_s'] = w_out[D_ATTN:].astype(BF16)
    W['g_ffn'] = row(P['g_ffn'])
    W['w_peer_q'] = P['w_peer_q'].astype(BF16)
    W['peer_keys'] = P['peer_keys'].reshape(2 * PEER_HEADS, N_KEYS, PEER_HALF).astype(BF16)
    W['peer_u'] = P['peer_u'].astype(BF16)
    W['peer_vt'] = P['peer_v'].T.astype(BF16)
    W['w_ple'] = P['w_ple'].astype(BF16)
    W['g_ple'] = row(P['g_ple'])
    W['w_ple_gate'] = P['w_ple_gate'].astype(BF16)
    return W


def _ffn_tail(attn, ssm, x, p, W, tm, tt_route, tt_peer):
    x1, xn, qp = _mix(attn, ssm, x, W, tm)
    n, e1, r2, e2 = _route(qp, W['peer_keys'], tt_route)
    y = _peer_dense(xn, W['peer_u'], W['peer_vt'], n, e1, r2, e2, tt_peer)
    return _final(x1, y, p, W, tm)


_PARAM_NAMES = ('g_mix', 'w_in', 'g_q_lat', 'w_uq', 'g_kv_lat', 'w_uk', 'w_uv', 'g_qn', 'g_kn', 'ssm_a_re',
                'ssm_a_im', 'ssm_log_dt', 'ssm_b_re', 'ssm_b_im', 'ssm_c_re', 'ssm_c_im', 'ssm_d', 'w_glu', 'b_glu',
                'g_attn_out', 'g_ssm_out', 'w_out', 'g_ffn', 'w_peer_q', 'peer_keys', 'peer_u', 'peer_v', 'g_ple',
                'w_ple', 'w_ple_gate')

TM_TOKEN = 512
TQ_FLASH = 1024
TK_FLASH = 1024
TS_FLASH = 512
TB_SCAN = 512
TT_ROUTE = 256
TT_PEER = 1024
PAGES_PER_STEP = 32
PAGES_PER_SEM = 8
DMA_ISSUE_UNROLL = 4


def _layer(x_p, x_s, p_p, p_s, caches, states, page_table, P):
    W = _prep_common(P)
    W.update(_prep_ssm(P))
    T = x_p.shape[0]
    B = x_s.shape[0]
    n_past = page_table.shape[1] * PAGE

    cos_t, sin_t = _rope_tables(jnp.arange(T))
    q, k, v, ckv_p, kr_p, ks_p, u = _project(x_p, cos_t, sin_t, W, BF16, TM_TOKEN)
    attn = _flash_attention(q, k, v, TQ_FLASH, TK_FLASH, TS_FLASH)
    ssm, hre_p, him_p = _s5_prompt(u, W, TB_SCAN)
    y_p = _ffn_tail(attn, ssm, x_p, p_p, W, TM_TOKEN, TT_ROUTE, TT_PEER)

    cos_s, sin_s = _rope_tables(jnp.full((B,), n_past, jnp.int32))
    q, _, _, ckv_s, kr_s, ks_s, u = _project(x_s, cos_s, sin_s, W, F32, B)
    attn = _sample_attention(q, ckv_s, kr_s, ks_s, page_table, *caches, W, PAGES_PER_STEP)
    ssm, hre_s, him_s = _s5_step(u, states[0].reshape(B, N_STATE), states[1].reshape(B, N_STATE), W)
    y_s = _ffn_tail(attn, ssm, x_s, p_s, W, B, B, B)
    st = lambda a, lead: a.reshape(lead + (N_GROUPS, SSM_STATE))
    return (y_p, y_s, (ckv_p, kr_p, ks_p, st(hre_p, (1,)), st(him_p, (1,))),
            (ckv_s, kr_s, ks_s, st(hre_s, (B,)), st(him_s, (B,))))


def kernel(x_prompt, x_sample, cache_kv_latent, cache_k_rope, cache_k_scale, state_ssm_re, state_ssm_im,
           page_table, p_prompt, p_sample, g_mix, w_in, g_q_lat, w_uq, g_kv_lat, w_uk, w_uv, g_qn, g_kn,
           ssm_a_re, ssm_a_im, ssm_log_dt, ssm_b_re, ssm_b_im, ssm_c_re, ssm_c_im, ssm_d, w_glu, b_glu,
           g_attn_out, g_ssm_out, w_out, g_ffn, w_peer_q, peer_keys, peer_u, peer_v, g_ple, w_ple, w_ple_gate):
    params = dict(zip(_PARAM_NAMES, (g_mix, w_in, g_q_lat, w_uq, g_kv_lat, w_uk, w_uv, g_qn, g_kn, ssm_a_re,
                                     ssm_a_im, ssm_log_dt, ssm_b_re, ssm_b_im, ssm_c_re, ssm_c_im, ssm_d, w_glu,
                                     b_glu, g_attn_out, g_ssm_out, w_out, g_ffn, w_peer_q, peer_keys, peer_u,
                                     peer_v, g_ple, w_ple, w_ple_gate)))
    depth = g_mix.shape[0]
    assert x_prompt.shape[0] == 1 and x_sample.shape[1] == 1
    x_p = x_prompt[0]
    x_s = x_sample[:, 0]
    outs_p, outs_s = [], []
    for i in range(depth):
        P = {name: val[i] for name, val in params.items()}
        x_p, x_s, o_p, o_s = _layer(x_p, x_s, p_prompt[i, 0], p_sample[i, :, 0],
                                    (cache_kv_latent[i], cache_k_rope[i], cache_k_scale[i]),
                                    (state_ssm_re[i], state_ssm_im[i]), page_table, P)
        outs_p.append(o_p)
        outs_s.append(o_s)
    stack_p = [jnp.stack([o[j] for o in outs_p])[:, None] for j in range(5)]
    stack_s = [jnp.stack([o[j] for o in outs_s]) for j in range(5)]
    stack_s = [a[:, :, None] if j < 3 else a for j, a in enumerate(stack_s)]
    return (x_p[None], x_s[:, None], stack_p[0], stack_p[1], stack_p[2], stack_p[3][:, 0], stack_p[4][:, 0],
            stack_s[0], stack_s[1], stack_s[2], stack_s[3], stack_s[4])
```

```python
import functools
import math

import jax
import jax.numpy as jnp
import numpy as np
from jax import lax
from jax.experimental import pallas as pl
from jax.experimental.pallas import tpu as pltpu

F32 = jnp.float32
BF16 = jnp.bfloat16

D_MODEL = 1024
N_HEADS = 8
QK_NOPE = 64
QK_ROPE = 32
QK_DIM = QK_NOPE + QK_ROPE
V_DIM = 64
D_ATTN = N_HEADS * V_DIM
Q_RANK = 384
KV_RANK = 256
ROPE_THETA = 10000.0
ATTN_SCALE = QK_DIM ** -0.5
D_SSM = 512
SSM_GROUP = 16
N_GROUPS = 32
SSM_STATE = 64
N_STATE = N_GROUPS * SSM_STATE
PEER_HEADS = 8
N_KEYS = 128
N_EXPERTS = N_KEYS * N_KEYS
PEER_TOPK = 16
PEER_HALF = 128
PLE_DIM = 256
PAGE = 128
EPS = 1e-6

LANES = 128
BF16_ROWS = 16
HP = N_HEADS * LANES
NEG = -1e30
VMEM_LIMIT = 56 * 1024 * 1024


def _cparams(sem, vmem=VMEM_LIMIT):
    return pltpu.CompilerParams(dimension_semantics=sem, vmem_limit_bytes=vmem)


def _dot(a, b):
    return jnp.dot(a, b, preferred_element_type=F32)


def _dot_nt(a, b):
    return lax.dot_general(a, b, (((1,), (1,)), ((), ())), preferred_element_type=F32)


def _rms(x, g, n):
    ms = jnp.sum(x * x, axis=-1, keepdims=True) * (1.0 / n)
    return x * lax.rsqrt(ms + EPS) * g


def _gelu(x):
    return 0.5 * x * (1.0 + jnp.tanh(0.7978845608028654 * (x + 0.044715 * (x * x * x))))


def _gelu_x2(x):
    c0 = 0.7978845608028654
    t = jnp.tanh(x * (c0 + (c0 * 0.044715) * (x * x)))
    return x + x * t


def _sigmoid(x):
    return 1.0 / (1.0 + jnp.exp(-x))


def _full(shape):
    return pl.BlockSpec(shape, lambda *_: (0,) * len(shape))


def _proj_kernel(x_ref, cos_ref, sin_ref, gmix_ref, win_ref, gq_ref, wuq_ref, wuqs_ref, gqn_ref, gqns_ref,
                 gkv_ref, wuk_ref, wuv_ref, gknn_ref, gkc_ref, gks_ref,
                 q_ref, k_ref, v_ref, ckv_ref, krope_ref, kscale_ref, u_ref):
    x = x_ref[...]
    cos = cos_ref[...]
    sin = sin_ref[...]
    h = _rms(x, gmix_ref[...], D_MODEL).astype(BF16)
    proj = _dot(h, win_ref[...])
    cq = proj[:, :Q_RANK]
    ckv = proj[:, Q_RANK:Q_RANK + KV_RANK]
    u_ref[...] = proj[:, 640:1152]
    krt = proj[:, 1152:1280]
    krts = proj[:, 1280:1408]

    cqn = _rms(cq, gq_ref[...], Q_RANK).astype(BF16)
    q_raw = _dot(cqn, wuq_ref[...])
    q_sw = _dot(cqn, wuqs_ref[...])
    gqn = gqn_ref[...]
    gqns = gqns_ref[...]

    ckvn = _rms(ckv, gkv_ref[...], KV_RANK)
    ckv_ref[...] = ckvn
    ckvb = ckvn.astype(BF16)
    kn_raw = _dot(ckvb, wuk_ref[...])
    v_ref[...] = _dot(ckvb, wuv_ref[...]).astype(v_ref.dtype)
    ms_rope = jnp.sum(krt * krt, axis=-1, keepdims=True)
    kr_tile = (krt * gkc_ref[...]) * cos + (krts * gks_ref[...]) * sin
    krope_ref[...] = kr_tile[:, QK_NOPE:QK_DIM]
    gknn = gknn_ref[...]
    lane = lax.broadcasted_iota(jnp.int32, (1, LANES), 1)
    ks_all = jnp.zeros((x.shape[0], LANES), F32)
    for hd in range(N_HEADS):
        sl = slice(hd * LANES, (hd + 1) * LANES)
        qh = q_raw[:, sl]
        rq = lax.rsqrt(jnp.sum(qh * qh, axis=-1, keepdims=True) * (1.0 / QK_DIM) + EPS)
        q_ref[:, sl] = ((qh * rq * gqn) * cos + (q_sw[:, sl] * rq * gqns) * sin).astype(q_ref.dtype)
        kh = kn_raw[:, sl]
        ks = lax.rsqrt((jnp.sum(kh * kh, axis=-1, keepdims=True) + ms_rope) * (1.0 / QK_DIM) + EPS)
        k_ref[:, sl] = ((kh * gknn + kr_tile) * ks).astype(k_ref.dtype)
        ks_all = jnp.where(lane == hd, ks, ks_all)
    kscale_ref[...] = ks_all[:, :N_HEADS]


def _project(x, cos_t, sin_t, W, q_dtype, tm):
    T = x.shape[0]
    tok = lambda w: pl.BlockSpec((tm, w), lambda i: (i, 0))
    weights = [W['g_mix'], W['w_in_p'], W['g_q_lat'], W['w_uq_p'], W['w_uq_sw'], W['g_qn_p'], W['g_qn_sw'],
               W['g_kv_lat'], W['w_uk_p'], W['w_uv_p'], W['g_kn_nope'], W['g_kc'], W['g_ks']]
    out_shape = [jax.ShapeDtypeStruct((T, HP), q_dtype), jax.ShapeDtypeStruct((T, HP), BF16),
                 jax.ShapeDtypeStruct((T, HP), BF16), jax.ShapeDtypeStruct((T, KV_RANK), F32),
                 jax.ShapeDtypeStruct((T, QK_ROPE), F32), jax.ShapeDtypeStruct((T, N_HEADS), F32),
                 jax.ShapeDtypeStruct((T, D_SSM), F32)]
    return pl.pallas_call(
        _proj_kernel, grid=(T // tm,),
        in_specs=[tok(D_MODEL), tok(LANES), tok(LANES)] + [_full(w.shape) for w in weights],
        out_specs=[tok(HP), tok(HP), tok(HP), tok(KV_RANK), tok(QK_ROPE), tok(N_HEADS), tok(D_SSM)],
        out_shape=out_shape, compiler_params=_cparams(("parallel",)), name="project",
    )(x, cos_t, sin_t, *weights)


SCORE_SCALE = ATTN_SCALE * math.log2(math.e)


def _flash_kernel(q_ref, k_ref, v_ref, o_ref, m_sc, l_sc, acc_sc, *, tq, tk, ts):
    i = pl.program_id(1)
    nsub = tq // ts
    m_sc[...] = jnp.full(m_sc.shape, NEG, F32)
    l_sc[...] = jnp.zeros(l_sc.shape, F32)
    acc_sc[...] = jnp.zeros(acc_sc.shape, F32)
    def sub_step(a, col0, k, v, masked):
        rows = slice(a * ts, (a + 1) * ts)
        width = k.shape[0]
        s = _dot_nt(q_ref[rows, :], k)
        if masked:
            r = i * tq + a * ts + lax.broadcasted_iota(jnp.int32, (ts, width), 0)
            c = col0 + lax.broadcasted_iota(jnp.int32, (ts, width), 1)
            s = jnp.where(c <= r, s, NEG)
        m_prev = m_sc[rows, :]
        m_new = jnp.maximum(m_prev, jnp.max(s, axis=-1, keepdims=True))
        alpha = jnp.exp2((m_prev - m_new) * SCORE_SCALE)
        p = jnp.exp2((s - jnp.concatenate([m_new] * (width // LANES), axis=1)) * SCORE_SCALE)
        l_sc[rows, :] = alpha * l_sc[rows, :] + jnp.sum(p, axis=-1, keepdims=True)
        acc_sc[rows, :] = alpha * acc_sc[rows, :] + _dot(p.astype(BF16), v)
        m_sc[rows, :] = m_new

    def full_step(j, carry):
        start = pl.multiple_of(j * tk, tk)
        k = k_ref[pl.ds(start, tk), :]
        v = v_ref[pl.ds(start, tk), :]
        for a in range(nsub):
            sub_step(a, j * tk, k, v, False)
        return carry

    n_full = i * (tq // tk)
    lax.fori_loop(0, n_full, full_step, 0)
    for jj in range(nsub):
        start = pl.multiple_of(i * tq + jj * ts, ts)
        k = k_ref[pl.ds(start, ts), :]
        v = v_ref[pl.ds(start, ts), :]
        for a in range(jj, nsub):
            sub_step(a, i * tq + jj * ts, k, v, a == jj)
    o_ref[...] = acc_sc[...] / l_sc[...]


def _flash_attention(q, k, v, tq, tk, ts):
    T = q.shape[0]
    kern = functools.partial(_flash_kernel, tq=tq, tk=tk, ts=ts)
    return pl.pallas_call(
        kern, grid=(N_HEADS, T // tq),
        in_specs=[pl.BlockSpec((tq, LANES), lambda h, i: (i, h)),
                  pl.BlockSpec((T, LANES), lambda h, i: (0, h)),
                  pl.BlockSpec((T, LANES), lambda h, i: (0, h))],
        out_specs=pl.BlockSpec((tq, LANES), lambda h, i: (i, h)),
        out_shape=jax.ShapeDtypeStruct((T, HP), F32),
        scratch_shapes=[pltpu.VMEM((tq, LANES), F32), pltpu.VMEM((tq, LANES), F32), pltpu.VMEM((tq, LANES), F32)],
        compiler_params=_cparams(("parallel", "arbitrary")), name="flash_attention",
    )(q, k, v)


def _cmul(ar, ai, br, bi):
    return ar * br - ai * bi, ar * bi + ai * br


def _ssm_disc_kernel(are_ref, aim_ref, dt_ref, bre_ref, bim_ref, abr_ref, abi_ref, bbr_ref, bbi_ref):
    lr = are_ref[...]
    li = aim_ref[...]
    dt = dt_ref[...]
    mag = jnp.exp(lr * dt)
    abr = mag * jnp.cos(li * dt)
    abi = mag * jnp.sin(li * dt)
    abr_ref[...] = abr
    abi_ref[...] = abi
    inv = 1.0 / (lr * lr + li * li)
    cr, ci = _cmul(abr - 1.0, abi, lr * inv, -li * inv)
    bbr, bbi = _cmul(cr, ci, bre_ref[...], bim_ref[...])
    bbr_ref[...] = bbr
    bbi_ref[...] = bbi


def _ssm_discretise(P):
    col = lambda a: a.reshape(N_STATE, 1)
    dt = jnp.repeat(jnp.exp(P['ssm_log_dt']), SSM_STATE).reshape(N_STATE, 1)
    bre = P['ssm_b_re'].reshape(N_STATE, SSM_GROUP)
    bim = P['ssm_b_im'].reshape(N_STATE, SSM_GROUP)
    s1 = jax.ShapeDtypeStruct((N_STATE, 1), F32)
    s16 = jax.ShapeDtypeStruct((N_STATE, SSM_GROUP), F32)
    return pl.pallas_call(_ssm_disc_kernel, out_shape=[s1, s1, s16, s16], name="ssm_discretise")(
        col(P['ssm_a_re']), col(P['ssm_a_im']), dt, bre, bim)


def _prep_ssm(P):
    abr, abi, bbr, bbi = _ssm_discretise(P)
    eye = jnp.eye(N_GROUPS, dtype=F32)

    def bdiag_in(b):
        b = b.reshape(N_GROUPS, SSM_STATE, SSM_GROUP)
        return jnp.einsum('gpc,gh->gchp', b, eye).reshape(D_SSM, N_STATE)

    def bdiag_out(c):
        return jnp.einsum('gcp,gh->gphc', c, eye).reshape(N_STATE, D_SSM)

    W = {}
    W['ssm_bm'] = jnp.concatenate([bdiag_in(bbr), bdiag_in(bbi)], axis=1)
    W['ssm_cm'] = jnp.concatenate([bdiag_out(P['ssm_c_re']), -bdiag_out(P['ssm_c_im'])], axis=0)
    W['ssm_ar'] = abr.reshape(1, N_STATE)
    W['ssm_ai'] = abi.reshape(1, N_STATE)
    W['ssm_d'] = P['ssm_d'].reshape(1, D_SSM)
    W['w_glu'] = P['w_glu'].astype(BF16)
    W['b_glu'] = P['b_glu'].reshape(1, D_SSM)
    return W


SCAN_LANES = 512
SCAN_UNROLL = 4


def _s5_prompt_kernel(u_ref, perm_ref, permt_ref, bm_ref, cm_ref, ar_ref, ai_ref, d_ref, wglu_ref, bglu_ref,
                      out_ref, hre_ref, him_ref, x_sc, carry_sc, *, tb):
    n = tb // 8
    nk = N_STATE // SCAN_LANES
    step = pl.program_id(0)

    @pl.when(step == 0)
    def _():
        carry_sc[...] = jnp.zeros(carry_sc.shape, F32)

    u = u_ref[...]
    up = _dot(perm_ref[...], u.astype(BF16)).astype(BF16)
    for k in range(nk):
        uk = up[:, k * LANES:(k + 1) * LANES]
        for c in range(2):
            lo = c * N_STATE + k * SCAN_LANES
            x_sc[:, lo:lo + SCAN_LANES] = _dot(uk, bm_ref[k * LANES:(k + 1) * LANES, lo:lo + SCAN_LANES])

    for k in range(nk):
        re_sl = slice(k * SCAN_LANES, (k + 1) * SCAN_LANES)
        im_sl = slice(N_STATE + k * SCAN_LANES, N_STATE + (k + 1) * SCAN_LANES)
        ar = jnp.broadcast_to(ar_ref[:, re_sl], (8, SCAN_LANES))
        ai = jnp.broadcast_to(ai_ref[:, re_sl], (8, SCAN_LANES))

        def local_scan(i, h):
            rows = pl.ds(pl.multiple_of(i * 8, 8), 8)
            pr, pi = _cmul(ar, ai, h[0], h[1])
            hr = pr + x_sc[rows, re_sl]
            hi = pi + x_sc[rows, im_sl]
            x_sc[rows, re_sl] = hr
            x_sc[rows, im_sl] = hi
            return hr, hi

        zero = jnp.zeros((8, SCAN_LANES), F32)
        er, ei = lax.fori_loop(0, n, local_scan, (zero, zero), unroll=SCAN_UNROLL)

        pr, pi = ar, ai
        for _ in range(int(math.log2(n))):
            pr, pi = _cmul(pr, pi, pr, pi)
        cr = [carry_sc[0:1, re_sl]]
        ci = [carry_sc[1:2, re_sl]]
        for s in range(1, 9):
            qr, qi = _cmul(pr[0:1], pi[0:1], cr[-1], ci[-1])
            cr.append(qr + er[s - 1:s])
            ci.append(qi + ei[s - 1:s])
        carry_sc[0:1, re_sl] = cr[8]
        carry_sc[1:2, re_sl] = ci[8]
        cin = (jnp.concatenate(cr[:8], axis=0), jnp.concatenate(ci[:8], axis=0))

        def patch(i, c):
            rows = pl.ds(pl.multiple_of(i * 8, 8), 8)
            c = _cmul(ar, ai, c[0], c[1])
            x_sc[rows, re_sl] = x_sc[rows, re_sl] + c[0]
            x_sc[rows, im_sl] = x_sc[rows, im_sl] + c[1]
            return c

        lax.fori_loop(0, n, patch, cin, unroll=SCAN_UNROLL)

    ys = []
    for k in range(nk):
        zr = x_sc[:, k * SCAN_LANES:(k + 1) * SCAN_LANES].astype(BF16)
        zi = x_sc[:, N_STATE + k * SCAN_LANES:N_STATE + (k + 1) * SCAN_LANES].astype(BF16)
        ys.append(_dot(zr, cm_ref[k * SCAN_LANES:(k + 1) * SCAN_LANES, k * LANES:(k + 1) * LANES])
                  + _dot(zi, cm_ref[N_STATE + k * SCAN_LANES:N_STATE + (k + 1) * SCAN_LANES,
                                    k * LANES:(k + 1) * LANES]))
    yp = jnp.concatenate(ys, axis=1)
    pt = permt_ref[...]
    hi = yp.astype(BF16)
    r1 = yp - hi.astype(F32)
    mid = r1.astype(BF16)
    lo = (r1 - mid.astype(F32)).astype(BF16)
    y = _dot(pt, hi) + _dot(pt, mid) + _dot(pt, lo) + d_ref[...] * u
    g = _gelu(y)
    out_ref[...] = g * _sigmoid(_dot(g.astype(BF16), wglu_ref[...]) + bglu_ref[...])
    hre_ref[...] = carry_sc[0:1, :]
    him_ref[...] = carry_sc[1:2, :]


def _s5_prompt(u, W, tb):
    T = u.shape[0]
    n = tb // 8
    r = np.arange(tb)
    src = (r % 8) * n + r // 8
    perm = np.zeros((tb, tb), np.float32)
    perm[r, src] = 1.0
    kern = functools.partial(_s5_prompt_kernel, tb=tb)
    weights = [jnp.asarray(perm, BF16), jnp.asarray(perm.T, BF16), W['ssm_bm'].astype(BF16),
               W['ssm_cm'].astype(BF16), W['ssm_ar'], W['ssm_ai'], W['ssm_d'], W['w_glu'], W['b_glu']]
    return pl.pallas_call(
        kern, grid=(T // tb,),
        in_specs=[pl.BlockSpec((tb, D_SSM), lambda i: (i, 0))] + [_full(w.shape) for w in weights],
        out_specs=[pl.BlockSpec((tb, D_SSM), lambda i: (i, 0)), _full((1, N_STATE)), _full((1, N_STATE))],
        out_shape=[jax.ShapeDtypeStruct((T, D_SSM), F32), jax.ShapeDtypeStruct((1, N_STATE), F32),
                   jax.ShapeDtypeStruct((1, N_STATE), F32)],
        scratch_shapes=[pltpu.VMEM((tb, 2 * N_STATE), F32), pltpu.VMEM((2, N_STATE), F32)],
        compiler_params=_cparams(("arbitrary",)), name="s5_prompt",
    )(u, *weights)


def _s5_step_kernel(u_ref, hre_ref, him_ref, bm_ref, cm_ref, ar_ref, ai_ref, d_ref, wglu_ref, bglu_ref,
                    out_ref, nre_ref, nim_ref):
    u = u_ref[...]
    bm = bm_ref[...]
    uh = u.astype(BF16)
    ul = (u - uh.astype(F32)).astype(BF16)
    bh = bm.astype(BF16)
    bl = (bm - bh.astype(F32)).astype(BF16)
    bu = _dot(uh, bh) + _dot(uh, bl) + _dot(ul, bh)
    pr, pi = _cmul(ar_ref[...], ai_ref[...], hre_ref[...], him_ref[...])
    hr = pr + bu[:, :N_STATE]
    hi = pi + bu[:, N_STATE:]
    nre_ref[...] = hr
    nim_ref[...] = hi
    cm = cm_ref[...]
    y = _dot(hr.astype(BF16), cm[:N_STATE]) + _dot(hi.astype(BF16), cm[N_STATE:]) + d_ref[...] * u
    g = _gelu(y)
    out_ref[...] = g * _sigmoid(_dot(g.astype(BF16), wglu_ref[...]) + bglu_ref[...])


def _s5_step(u, hre, him, W):
    B = u.shape[0]
    st = jax.ShapeDtypeStruct((B, N_STATE), F32)
    return pl.pallas_call(
        _s5_step_kernel, out_shape=[jax.ShapeDtypeStruct((B, D_SSM), F32), st, st],
        compiler_params=pltpu.CompilerParams(vmem_limit_bytes=VMEM_LIMIT), name="s5_step",
    )(u, hre, him, W['ssm_bm'], W['ssm_cm'].astype(BF16), W['ssm_ar'], W['ssm_ai'], W['ssm_d'],
      W['w_glu'], W['b_glu'])


def _qlat_kernel(q_ref, gknn_ref, wukt_ref, o_ref):
    g = gknn_ref[...]
    for h in range(N_HEADS):
        qh = (q_ref[:, h * LANES:(h + 1) * LANES] * g).astype(BF16)
        o_ref[:, h * KV_RANK:(h + 1) * KV_RANK] = _dot(qh, wukt_ref[h]).astype(o_ref.dtype)


def _uv_kernel(ol_ref, wuv_ref, o_ref):
    for h in range(N_HEADS):
        o_ref[:, h * LANES:(h + 1) * LANES] = _dot(ol_ref[:, h * KV_RANK:(h + 1) * KV_RANK].astype(BF16),
                                                   wuv_ref[:, h * LANES:(h + 1) * LANES])


def _decode_kernel(pt_ref, qlat_ref, qr_ref, ckv_ref, krn_ref, ksn_ref, lat_hbm, kr_hbm, ks_hbm, o_ref,
                   lat_buf, kr_buf, ks_buf, sem, m_sc, l_sc, acc_sc, *, pg, nch):
    b = pl.program_id(0)
    c = pl.program_id(1)
    g = b * nch + c
    total = pl.num_programs(0) * nch

    def copies(step, slot, p):
        page = pt_ref[step * pg + p]
        grp = p // PAGES_PER_SEM
        return (pltpu.make_async_copy(lat_hbm.at[page], lat_buf.at[slot, pl.ds(p * PAGE, PAGE)],
                                      sem.at[slot, 0, grp]),
                pltpu.make_async_copy(kr_hbm.at[page], kr_buf.at[slot, p], sem.at[slot, 1, grp]),
                pltpu.make_async_copy(ks_hbm.at[page], ks_buf.at[slot, p], sem.at[slot, 2, grp]))

    n_groups = pl.cdiv(pg, PAGES_PER_SEM)

    def group_pages(grp):
        first = grp * PAGES_PER_SEM
        return first, min(PAGES_PER_SEM, pg - first)

    def start_group(step, slot, grp):
        first, count = group_pages(grp)

        def body(p, carry):
            for d in copies(step, slot, p):
                d.start()
            return carry
        lax.fori_loop(first, first + count, body, 0, unroll=DMA_ISSUE_UNROLL)

    def wait_group(slot, grp):
        first, count = group_pages(grp)
        lat_grp = lat_buf.at[slot, pl.ds(first * PAGE, count * PAGE)]
        kr_grp = kr_buf.at[slot, pl.ds(first, count)]
        ks_grp = ks_buf.at[slot, pl.ds(first, count)]
        pltpu.make_async_copy(lat_grp, lat_grp, sem.at[slot, 0, grp]).wait()
        pltpu.make_async_copy(kr_grp, kr_grp, sem.at[slot, 1, grp]).wait()
        pltpu.make_async_copy(ks_grp, ks_grp, sem.at[slot, 2, grp]).wait()

    @pl.when(g == 0)
    def _():
        for grp in range(n_groups):
            start_group(0, 0, grp)

    @pl.when(c == 0)
    def _():
        m_sc[...] = jnp.full(m_sc.shape, NEG, F32)
        l_sc[...] = jnp.zeros(l_sc.shape, F32)
        acc_sc[...] = jnp.zeros(acc_sc.shape, F32)

    slot = g % 2
    for grp in range(n_groups):
        wait_group(slot, grp)

        @pl.when(g + 1 < total)
        def _():
            start_group(g + 1, 1 - slot, grp)

    qlat = qlat_ref[0]
    lat = lat_buf[slot].astype(BF16)
    kr = jnp.concatenate([kr_buf[slot, p] for p in range(pg)], axis=1).astype(BF16)
    ks_t = jnp.concatenate([ks_buf[slot, p] for p in range(pg)], axis=1)
    s = _dot_nt(qlat, lat) + _dot(qr_ref[0], kr)
    s = s * ks_t * ATTN_SCALE
    m_prev = m_sc[...]
    m_new = jnp.maximum(m_prev, jnp.max(s, axis=-1, keepdims=True))
    alpha = jnp.exp(m_prev - m_new)
    p = jnp.exp(s - m_new)
    l_sc[...] = alpha * l_sc[...] + jnp.sum(p, axis=-1, keepdims=True)
    acc_sc[...] = alpha * acc_sc[...] + _dot(p.astype(BF16), lat)
    m_sc[...] = m_new

    @pl.when(c == nch - 1)
    def _():
        ckv = ckv_ref[0]
        s_new = (jnp.sum(qlat.astype(F32) * ckv, axis=-1, keepdims=True)
                 + jnp.sum(qr_ref[0].astype(F32) * krn_ref[0], axis=-1, keepdims=True))
        s_new = s_new * ksn_ref[0] * ATTN_SCALE
        m_prev = m_sc[...]
        m_new = jnp.maximum(m_prev, s_new)
        alpha = jnp.exp(m_prev - m_new)
        p_new = jnp.exp(s_new - m_new)
        l = alpha * l_sc[...] + p_new
        o_ref[0] = (alpha * acc_sc[...] + p_new * ckv) / l


def _decode_attention(page_table, qlat, qr, ckv, krn, ksn, lat_pool, kr_pool, ks_pool, pg):
    B, n_pages = page_table.shape
    nch = n_pages // pg
    tk = pg * PAGE
    kern = functools.partial(_decode_kernel, pg=pg, nch=nch)
    per_b = lambda shape: pl.BlockSpec((1,) + shape, lambda b, c, pt: (b, 0, 0))
    grid_spec = pltpu.PrefetchScalarGridSpec(
        num_scalar_prefetch=1, grid=(B, nch),
        in_specs=[per_b((N_HEADS, KV_RANK)), per_b((N_HEADS, QK_ROPE)), per_b((1, KV_RANK)), per_b((1, QK_ROPE)),
                  per_b((N_HEADS, 1)),
                  pl.BlockSpec(memory_space=pl.ANY), pl.BlockSpec(memory_space=pl.ANY),
                  pl.BlockSpec(memory_space=pl.ANY)],
        out_specs=per_b((N_HEADS, KV_RANK)),
        scratch_shapes=[pltpu.VMEM((2, tk, KV_RANK), F32), pltpu.VMEM((2, pg, QK_ROPE, PAGE), F32),
                        pltpu.VMEM((2, pg, N_HEADS, PAGE), F32),
                        pltpu.SemaphoreType.DMA((2, 3, pl.cdiv(pg, PAGES_PER_SEM))),
                        pltpu.VMEM((N_HEADS, 1), F32), pltpu.VMEM((N_HEADS, 1), F32),
                        pltpu.VMEM((N_HEADS, KV_RANK), F32)])
    return pl.pallas_call(
        kern, grid_spec=grid_spec, out_shape=jax.ShapeDtypeStruct((B, N_HEADS, KV_RANK), F32),
        compiler_params=_cparams(("arbitrary", "arbitrary")), name="decode_attention",
    )(page_table.reshape(-1), qlat, qr, ckv, krn, ksn, lat_pool, kr_pool, ks_pool)


def _sample_attention(q, ckv, krope, kscale, page_table, lat_pool, kr_pool, ks_pool, W, pg):
    B = q.shape[0]
    qlat = pl.pallas_call(_qlat_kernel, out_shape=jax.ShapeDtypeStruct((B, N_HEADS * KV_RANK), BF16),
                          name="q_absorb")(q, W['g_kn_nope'], W['w_uk_t'])
    qr = q.reshape(B, N_HEADS, LANES)[:, :, QK_NOPE:QK_DIM].astype(BF16)
    ol = _decode_attention(page_table, qlat.reshape(B, N_HEADS, KV_RANK), qr, ckv.reshape(B, 1, KV_RANK),
                           krope.reshape(B, 1, QK_ROPE), kscale.reshape(B, N_HEADS, 1),
                           lat_pool, jnp.swapaxes(kr_pool, 1, 2), jnp.swapaxes(ks_pool, 1, 2), pg)
    return pl.pallas_call(_uv_kernel, out_shape=jax.ShapeDtypeStruct((B, HP), F32),
                          name="value_up")(ol.reshape(B, N_HEADS * KV_RANK), W['w_uv_p'])


def _mix_kernel(attn_ref, ssm_ref, x_ref, ga_ref, gs_ref, woa_ref, wos_ref, gffn_ref, wq_ref,
                x1_ref, xn_ref, qp_ref):
    a = _rms(attn_ref[...], ga_ref[...], D_ATTN).astype(BF16)
    s = _rms(ssm_ref[...], gs_ref[...], D_SSM).astype(BF16)
    x1 = x_ref[...] + _dot(a, woa_ref[...]) + _dot(s, wos_ref[...])
    x1_ref[...] = x1
    xn = _rms(x1, gffn_ref[...], D_MODEL).astype(BF16)
    xn_ref[...] = xn
    qp_ref[...] = _dot(xn, wq_ref[...]).astype(BF16)


def _mix(attn, ssm, x, W, tm):
    T = x.shape[0]
    tok = lambda w: pl.BlockSpec((tm, w), lambda i: (i, 0))
    weights = [W['g_attn_p'], W['g_ssm_out'], W['w_out_a'], W['w_out_s'], W['g_ffn'], W['w_peer_q']]
    nq = 2 * PEER_HEADS * PEER_HALF
    return pl.pallas_call(
        _mix_kernel, grid=(T // tm,),
        in_specs=[tok(HP), tok(D_SSM), tok(D_MODEL)] + [_full(w.shape) for w in weights],
        out_specs=[tok(D_MODEL), tok(D_MODEL), tok(nq)],
        out_shape=[jax.ShapeDtypeStruct((T, D_MODEL), F32), jax.ShapeDtypeStruct((T, D_MODEL), BF16),
                   jax.ShapeDtypeStruct((T, nq), BF16)],
        compiler_params=_cparams(("parallel",)), name="mix",
    )(attn, ssm, x, *weights)


_CAND = [(a, b) for a in range(PEER_TOPK) for b in range(PEER_TOPK) if (a + 1) * (b + 1) <= PEER_TOPK]


def _top16_rows(s):
    vals = []
    for _ in range(PEER_TOPK):
        m = jnp.max(s, axis=0, keepdims=True)
        vals.append(m)
        s = jnp.where(s == m, NEG, s)
    return vals


def _top16_ranked(s):
    vals = []
    rank = jnp.full(s.shape, float(PEER_TOPK), F32)
    for k in range(PEER_TOPK):
        m = jnp.max(s, axis=0, keepdims=True)
        vals.append(m)
        hit = s == m
        rank = jnp.where(hit, float(k), rank)
        s = jnp.where(hit, NEG, s)
    return vals, rank


def _route_kernel(qp_ref, keys_ref, n_ref, e1_ref, r2_ref, e2_ref):
    for h in range(PEER_HEADS):
        q1 = qp_ref[:, (2 * h) * PEER_HALF:(2 * h + 1) * PEER_HALF]
        q2 = qp_ref[:, (2 * h + 1) * PEER_HALF:(2 * h + 2) * PEER_HALF]
        s1 = _dot_nt(keys_ref[2 * h], q1)
        s2 = _dot_nt(keys_ref[2 * h + 1], q2)
        v1 = _top16_rows(s1)
        v2, r2 = _top16_ranked(s2)
        cand = [v1[a] + v2[b] for a, b in _CAND]
        call = jnp.concatenate(cand, axis=0)
        tau = _top16_rows(call)[PEER_TOPK - 1]
        cmax = cand[0]
        z = jnp.sum(jnp.where(call >= tau, jnp.exp(call - cmax), 0.0), axis=0, keepdims=True)
        n = jnp.zeros(s1.shape, F32)
        for k2 in range(PEER_TOPK):
            n = n + jnp.where(s1 + v2[k2] >= tau, 1.0, 0.0)
        n_ref[h] = n
        e1_ref[h] = jnp.exp(s1 - v1[0]) * (0.5 / z)
        r2_ref[h] = r2.astype(BF16)
        e2_ref[h] = jnp.exp(s2 - v2[0]).astype(BF16)


def _route(qp, keys, tt):
    T = qp.shape[0]
    big = lambda dt: jax.ShapeDtypeStruct((PEER_HEADS, N_KEYS, T), dt)
    bspec = pl.BlockSpec((PEER_HEADS, N_KEYS, tt), lambda i: (0, 0, i))
    return pl.pallas_call(
        _route_kernel, grid=(T // tt,),
        in_specs=[pl.BlockSpec((tt, qp.shape[1]), lambda i: (i, 0)), _full(keys.shape)],
        out_specs=[bspec, bspec, bspec, bspec],
        out_shape=[big(F32), big(F32), big(BF16), big(BF16)],
        compiler_params=_cparams(("parallel",)), name="peer_route",
    )(qp, keys)


ROWS_PER_STEP = 16
E_TILE = ROWS_PER_STEP * N_KEYS


def _peer_kernel(xn_ref, u_ref, vt_ref, n_ref, e1_ref, r2_ref, e2_ref, y_ref, acc_sc, at_sc):
    j = pl.program_id(1)

    @pl.when(j == 0)
    def _():
        acc_sc[...] = jnp.zeros(acc_sc.shape, F32)

    st = _dot_nt(u_ref[...], xn_ref[...])
    tt = st.shape[1]
    zero = jnp.zeros((), BF16)
    groups = N_KEYS // BF16_ROWS

    def row16(ref, h, r):
        return jnp.broadcast_to(ref[h, r:r + 1, :], (BF16_ROWS, tt)).astype(BF16)[None]

    for r in range(ROWS_PER_STEP):
        w = None
        for h in range(PEER_HEADS):
            r2 = r2_ref[h].reshape(groups, BF16_ROWS, tt)
            e2 = e2_ref[h].reshape(groups, BF16_ROWS, tt)
            g = jnp.where(r2 < row16(n_ref, h, r), e2 * row16(e1_ref, h, r), zero)
            w = g if w is None else w + g
        rows = slice(r * N_KEYS, (r + 1) * N_KEYS)
        at_sc[rows, :] = _gelu_x2(st[rows, :]).astype(BF16) * w.reshape(N_KEYS, tt)
    acc_sc[...] += _dot(vt_ref[...], at_sc[...])

    @pl.when(j == pl.num_programs(1) - 1)
    def _():
        y_ref[...] = acc_sc[...].T


def _peer_dense(xn, u_tab, vt_tab, n, e1, r2, e2, tt):
    T = xn.shape[0]
    rows = pl.BlockSpec((PEER_HEADS, ROWS_PER_STEP, tt), lambda i, j: (0, j, i))
    keys = pl.BlockSpec((PEER_HEADS, N_KEYS, tt), lambda i, j: (0, 0, i))
    return pl.pallas_call(
        _peer_kernel, grid=(T // tt, N_EXPERTS // E_TILE),
        in_specs=[pl.BlockSpec((tt, D_MODEL), lambda i, j: (i, 0)),
                  pl.BlockSpec((E_TILE, D_MODEL), lambda i, j: (j, 0)),
                  pl.BlockSpec((D_MODEL, E_TILE), lambda i, j: (0, j)),
                  rows, rows, keys, keys],
        out_specs=pl.BlockSpec((tt, D_MODEL), lambda i, j: (i, 0)),
        out_shape=jax.ShapeDtypeStruct((T, D_MODEL), F32),
        scratch_shapes=[pltpu.VMEM((D_MODEL, tt), F32), pltpu.VMEM((E_TILE, tt), BF16)],
        compiler_params=_cparams(("parallel", "arbitrary")), name="peer_experts",
    )(xn, u_tab, vt_tab, n, e1, r2, e2)


def _final_kernel(x1_ref, y_ref, p_ref, wple_ref, gple_ref, wgate_ref, o_ref):
    x2 = x1_ref[...] + y_ref[...]
    e = _dot(p_ref[...].astype(BF16), wple_ref[...])
    gate = _sigmoid(_dot(_rms(x2, gple_ref[...], D_MODEL).astype(BF16), wgate_ref[...]))
    o_ref[...] = x2 + e * gate


def _final(x1, y, p, W, tm):
    T = x1.shape[0]
    tok = lambda w: pl.BlockSpec((tm, w), lambda i: (i, 0))
    weights = [W['w_ple'], W['g_ple'], W['w_ple_gate']]
    return pl.pallas_call(
        _final_kernel, grid=(T // tm,),
        in_specs=[tok(D_MODEL), tok(D_MODEL), tok(PLE_DIM)] + [_full(w.shape) for w in weights],
        out_specs=tok(D_MODEL), out_shape=jax.ShapeDtypeStruct((T, D_MODEL), F32),
        compiler_params=_cparams(("parallel",)), name="ple_gate",
    )(x1, y, p, *weights)


def _pad_heads(w, width):
    pad = [(0, 0)] * (w.ndim - 1) + [(0, LANES - width)]
    w = jnp.pad(w, pad)
    return w.reshape(w.shape[:-2] + (HP,))


def _rope_tables(pos):
    inv = ROPE_THETA ** (-jnp.arange(0, QK_ROPE, 2, dtype=F32) / QK_ROPE)
    ang = pos.astype(F32)[:, None] * inv[None, :]
    cos, sin = jnp.cos(ang), jnp.sin(ang)
    T = pos.shape[0]
    cos_t = jnp.concatenate([jnp.ones((T, QK_NOPE), F32), cos, cos, jnp.zeros((T, LANES - QK_DIM), F32)], axis=1)
    sin_t = jnp.concatenate([jnp.zeros((T, QK_NOPE), F32), -sin, sin, jnp.zeros((T, LANES - QK_DIM), F32)], axis=1)
    return cos_t, sin_t


def _swap_rope(w):
    z = jnp.zeros_like(w[..., :QK_NOPE])
    return jnp.concatenate([z, w[..., 80:96], w[..., 64:80]], axis=-1)


def _prep_common(P):
    W = {}
    row = lambda g: g.reshape(1, -1).astype(F32)
    w_in = P['w_in']
    kr = w_in[:, 640:672]
    z64 = jnp.zeros((D_MODEL, QK_NOPE), F32)
    z32 = jnp.zeros((D_MODEL, LANES - QK_DIM), F32)
    W['w_in_p'] = jnp.concatenate([w_in[:, :640], w_in[:, 672:], z64, kr, z32,
                                   z64, kr[:, 16:], kr[:, :16], z32], axis=1).astype(BF16)
    W['g_mix'] = row(P['g_mix'])
    W['g_q_lat'] = row(P['g_q_lat'])
    W['w_uq_p'] = _pad_heads(P['w_uq'], QK_DIM).astype(BF16)
    W['w_uq_sw'] = _pad_heads(_swap_rope(P['w_uq']), QK_DIM).astype(BF16)
    g_qn = P['g_qn']
    W['g_qn_p'] = jnp.pad(g_qn, (0, LANES - QK_DIM)).reshape(1, LANES)
    W['g_qn_sw'] = jnp.pad(_swap_rope(g_qn), (0, LANES - QK_DIM)).reshape(1, LANES)
    W['g_kv_lat'] = row(P['g_kv_lat'])
    W['w_uk_p'] = _pad_heads(P['w_uk'], QK_NOPE).astype(BF16)
    W['w_uv_p'] = _pad_heads(P['w_uv'], V_DIM).astype(BF16)
    W['w_uk_t'] = jnp.pad(jnp.transpose(P['w_uk'], (1, 2, 0)),
                          ((0, 0), (0, LANES - QK_NOPE), (0, 0))).astype(BF16)
    g_kn = P['g_kn']
    W['g_kn_nope'] = jnp.pad(g_kn[:QK_NOPE], (0, LANES - QK_NOPE)).reshape(1, LANES)
    W['g_kc'] = jnp.pad(g_kn, (0, LANES - QK_DIM)).at[:QK_NOPE].set(0.0).reshape(1, LANES)
    W['g_ks'] = jnp.pad(_swap_rope(g_kn), (0, LANES - QK_DIM)).reshape(1, LANES)

    W['g_attn_p'] = _pad_heads(P['g_attn_out'].reshape(N_HEADS, V_DIM), V_DIM).reshape(1, HP)
    W['g_ssm_out'] = row(P['g_ssm_out'])
    w_out = P['w_out']
    W['w_out_a'] = jnp.pad(w_out[:D_ATTN].reshape(N_HEADS, V_DIM, D_MODEL),
                           ((0, 0), (0, LANES - V_DIM), (0, 0))).reshape(HP, D_MODEL).astype(BF16)
    W['w_out_s'] = w_out[D_ATTN:].astype(BF16)
    W['g_ffn'] = row(P['g_ffn'])
    W['w_peer_q'] = P['w_peer_q'].astype(BF16)
    W['peer_keys'] = P['peer_keys'].reshape(2 * PEER_HEADS, N_KEYS, PEER_HALF).astype(BF16)
    W['peer_u'] = P['peer_u'].astype(BF16)
    W['peer_vt'] = P['peer_v'].T.astype(BF16)
    W['w_ple'] = P['w_ple'].astype(BF16)
    W['g_ple'] = row(P['g_ple'])
    W['w_ple_gate'] = P['w_ple_gate'].astype(BF16)
    return W


def _ffn_tail(attn, ssm, x, p, W, tm, tt_route, tt_peer):
    x1, xn, qp = _mix(attn, ssm, x, W, tm)
    n, e1, r2, e2 = _route(qp, W['peer_keys'], tt_route)
    y = _peer_dense(xn, W['peer_u'], W['peer_vt'], n, e1, r2, e2, tt_peer)
    return _final(x1, y, p, W, tm)


_PARAM_NAMES = ('g_mix', 'w_in', 'g_q_lat', 'w_uq', 'g_kv_lat', 'w_uk', 'w_uv', 'g_qn', 'g_kn', 'ssm_a_re',
                'ssm_a_im', 'ssm_log_dt', 'ssm_b_re', 'ssm_b_im', 'ssm_c_re', 'ssm_c_im', 'ssm_d', 'w_glu', 'b_glu',
                'g_attn_out', 'g_ssm_out', 'w_out', 'g_ffn', 'w_peer_q', 'peer_keys', 'peer_u', 'peer_v', 'g_ple',
                'w_ple', 'w_ple_gate')

TM_TOKEN = 512
TQ_FLASH = 1024
TK_FLASH = 1024
TS_FLASH = 512
TB_SCAN = 512
TT_ROUTE = 256
TT_PEER = 1024
PAGES_PER_STEP = 32
PAGES_PER_SEM = 8
DMA_ISSUE_UNROLL = 4


def _layer(x_p, x_s, p_p, p_s, caches, states, page_table, P):
    W = _prep_common(P)
    W.update(_prep_ssm(P))
    T = x_p.shape[0]
    B = x_s.shape[0]
    n_past = page_table.shape[1] * PAGE

    cos_t, sin_t = _rope_tables(jnp.arange(T))
    q, k, v, ckv_p, kr_p, ks_p, u = _project(x_p, cos_t, sin_t, W, BF16, TM_TOKEN)
    attn = _flash_attention(q, k, v, TQ_FLASH, TK_FLASH, TS_FLASH)
    ssm, hre_p, him_p = _s5_prompt(u, W, TB_SCAN)
    y_p = _ffn_tail(attn, ssm, x_p, p_p, W, TM_TOKEN, TT_ROUTE, TT_PEER)

    cos_s, sin_s = _rope_tables(jnp.full((B,), n_past, jnp.int32))
    q, _, _, ckv_s, kr_s, ks_s, u = _project(x_s, cos_s, sin_s, W, F32, B)
    attn = _sample_attention(q, ckv_s, kr_s, ks_s, page_table, *caches, W, PAGES_PER_STEP)
    ssm, hre_s, him_s = _s5_step(u, states[0].reshape(B, N_STATE), states[1].reshape(B, N_STATE), W)
    y_s = _ffn_tail(attn, ssm, x_s, p_s, W, B, B, B)
    st = lambda a, lead: a.reshape(lead + (N_GROUPS, SSM_STATE))
    return (y_p, y_s, (ckv_p, kr_p, ks_p, st(hre_p, (1,)), st(him_p, (1,))),
            (ckv_s, kr_s, ks_s, st(hre_s, (B,)), st(him_s, (B,))))


def kernel(x_prompt, x_sample, cache_kv_latent, cache_k_rope, cache_k_scale, state_ssm_re, state_ssm_im,
           page_table, p_prompt, p_sample, g_mix, w_in, g_q_lat, w_uq, g_kv_lat, w_uk, w_uv, g_qn, g_kn,
           ssm_a_re, ssm_a_im, ssm_log_dt, ssm_b_re, ssm_b_im, ssm_c_re, ssm_c_im, ssm_d, w_glu, b_glu,
           g_attn_out, g_ssm_out, w_out, g_ffn, w_peer_q, peer_keys, peer_u, peer_v, g_ple, w_ple, w_ple_gate):
    params = dict(zip(_PARAM_NAMES, (g_mix, w_in, g_q_lat, w_uq, g_kv_lat, w_uk, w_uv, g_qn, g_kn, ssm_a_re,
                                     ssm_a_im, ssm_log_dt, ssm_b_re, ssm_b_im, ssm_c_re, ssm_c_im, ssm_d, w_glu,
                                     b_glu, g_attn_out, g_ssm_out, w_out, g_ffn, w_peer_q, peer_keys, peer_u,
                                     peer_v, g_ple, w_ple, w_ple_gate)))
    depth = g_mix.shape[0]
    assert x_prompt.shape[0] == 1 and x_sample.shape[1] == 1
    x_p = x_prompt[0]
    x_s = x_sample[:, 0]
    outs_p, outs_s = [], []
    for i in range(depth):
        P = {name: val[i] for name, val in params.items()}
        x_p, x_s, o_p, o_s = _layer(x_p, x_s, p_prompt[i, 0], p_sample[i, :, 0],
                                    (cache_kv_latent[i], cache_k_rope[i], cache_k_scale[i]),
                                    (state_ssm_re[i], state_ssm_im[i]), page_table, P)
        outs_p.append(o_p)
        outs_s.append(o_s)
    stack_p = [jnp.stack([o[j] for o in outs_p])[:, None] for j in range(5)]
    stack_s = [jnp.stack([o[j] for o in outs_s]) for j in range(5)]
    stack_s = [a[:, :, None] if j < 3 else a for j, a in enumerate(stack_s)]
    return (x_p[None], x_s[:, None], stack_p[0], stack_p[1], stack_p[2], stack_p[3][:, 0], stack_p[4][:, 0],
            stack_s[0], stack_s[1], stack_s[2], stack_s[3], stack_s[4])
```
